```python
import jax, jax.numpy as jnp
from jax import lax
import numpy as np

D_MODEL = 1024
BATCH = 1
SEQ = 16384
DEPTH = 2

GRID_W = 64
N_FOURIER_GROUPS = 4
FOURIER_GROUP_DIM = D_MODEL // 8
FOURIER_WIDTH = N_FOURIER_GROUPS * FOURIER_GROUP_DIM
CHUNK = 128
N_GATE_GROUPS = 4
GATE_GROUP_DIM = D_MODEL // 8
GATE_WIDTH = N_GATE_GROUPS * GATE_GROUP_DIM
AB_IN_WIDTH = FOURIER_WIDTH + 2 * GATE_WIDTH
AB_OUT_WIDTH = FOURIER_WIDTH + GATE_WIDTH
HEAD_DIM = 128
N_HEADS = D_MODEL // HEAD_DIM
N_KV_HEADS = 2
KV_GROUP = N_HEADS // N_KV_HEADS
QKV_WIDTH = (N_HEADS + 2 * N_KV_HEADS) * HEAD_DIM
Q_BLOCK = 128
ROPE_THETA = 10000.0
ROPE_FREQS_PER_AXIS = HEAD_DIM // 4
D_FF = ((8 * D_MODEL // 3 + 255) // 256) * 256
EPS = 1e-6
N_EVEN = (DEPTH + 1) // 2
N_ODD = DEPTH // 2

kernel_name = "hybrid_fourier_gmlp_axial_gqa_macaron_encoder"


def rmsnorm(x, g):
    xf = x.astype(jnp.float32)
    y = xf * lax.rsqrt(jnp.mean(xf * xf, axis=-1, keepdims=True) + EPS)
    return (y * g.astype(jnp.float32)).astype(x.dtype)


def swiglu(h, w_gate, w_up, w_down):
    return (jax.nn.silu(h @ w_gate) * (h @ w_up)) @ w_down


def fourier_gating_mixer(h, w_in, v_norm, w_s, b_s, w_out):
    B, S, _ = h.shape
    z = h @ w_in
    f = z[..., :FOURIER_WIDTH].reshape(B, S, N_FOURIER_GROUPS, FOURIER_GROUP_DIM)
    f = jnp.fft.fft2(f.astype(jnp.float32), axes=(1, 3), norm="ortho").real
    f = f.astype(h.dtype).reshape(B, S, FOURIER_WIDTH)
    uv = jax.nn.gelu(z[..., FOURIER_WIDTH:])
    u = uv[..., :GATE_WIDTH]
    v = rmsnorm(uv[..., GATE_WIDTH:].reshape(B, S, N_GATE_GROUPS, GATE_GROUP_DIM), v_norm)
    vc = v.reshape(B, S // CHUNK, CHUNK, N_GATE_GROUPS, GATE_GROUP_DIM)
    s = jnp.einsum('gpq,bcqgd->bcpgd', w_s, vc) + b_s.T[None, None, :, :, None]
    gated = u * s.reshape(B, S, GATE_WIDTH)
    return jnp.concatenate([f, gated], axis=-1) @ w_out


def axial_rope_tables(S):
    rows = S // GRID_W
    row_idx = jnp.broadcast_to(jnp.arange(rows)[:, None], (rows, GRID_W)).reshape(S)
    col_idx = jnp.broadcast_to(jnp.arange(GRID_W)[None, :], (rows, GRID_W)).reshape(S)
    inv_freq = ROPE_THETA ** (-jnp.arange(ROPE_FREQS_PER_AXIS, dtype=jnp.float32) / ROPE_FREQS_PER_AXIS)
    ang = jnp.concatenate([row_idx.astype(jnp.float32)[:, None] * inv_freq[None, :],
                           col_idx.astype(jnp.float32)[:, None] * inv_freq[None, :]], axis=-1)
    return jnp.cos(ang), jnp.sin(ang)


def apply_rope(x, cos, sin):
    B, S, H, D = x.shape
    xp = x.astype(jnp.float32).reshape(B, S, H, D // 2, 2)
    x0, x1 = xp[..., 0], xp[..., 1]
    c = cos[None, :, None, :]
    s = sin[None, :, None, :]
    y = jnp.stack([x0 * c - x1 * s, x0 * s + x1 * c], axis=-1)
    return y.reshape(B, S, H, D).astype(x.dtype)


def axial_gqa(h, w_qkv, q_norm, k_norm, w_o):
    B, S, _ = h.shape
    qkv = h @ w_qkv
    q = qkv[..., :N_HEADS * HEAD_DIM].reshape(B, S, N_HEADS, HEAD_DIM)
    k = qkv[..., N_HEADS * HEAD_DIM:(N_HEADS + N_KV_HEADS) * HEAD_DIM].reshape(B, S, N_KV_HEADS, HEAD_DIM)
    v = qkv[..., (N_HEADS + N_KV_HEADS) * HEAD_DIM:].reshape(B, S, N_KV_HEADS, HEAD_DIM)
    cos, sin = axial_rope_tables(S)
    q = apply_rope(rmsnorm(q, q_norm), cos, sin) * (HEAD_DIM ** -0.5)
    k = apply_rope(rmsnorm(k, k_norm), cos, sin)
    nb = S // Q_BLOCK
    qb = q.reshape(B, nb, Q_BLOCK, N_KV_HEADS, KV_GROUP, HEAD_DIM).transpose(1, 0, 2, 3, 4, 5)

    def block(q_blk):
        sc = jnp.einsum('bqkgd,bskd->bkgqs', q_blk, k, preferred_element_type=jnp.float32)
        p = jax.nn.softmax(sc, axis=-1)
        return jnp.einsum('bkgqs,bskd->bqkgd', p.astype(v.dtype), v)

    o = lax.map(block, qb)
    o = o.transpose(1, 0, 2, 3, 4, 5).reshape(B, S, N_HEADS * HEAD_DIM)
    return o @ w_o


def setup_inputs(seed: int = 0) -> dict:
    key = jax.random.key(seed)
    ks = jax.random.split(key, 24)
    f32 = jnp.float32

    def w(k, shape, fan_in):
        return jax.random.normal(k, shape, f32) * (fan_in ** -0.5)

    def gain(k, shape):
        return 1.0 + 0.02 * jax.random.normal(k, shape, f32)

    return {
        "x": jax.random.normal(ks[0], (BATCH, SEQ, D_MODEL), f32),
        "ffn1_norm": gain(ks[1], (DEPTH, D_MODEL)),
        "ffn1_w_gate": w(ks[2], (DEPTH, D_MODEL, D_FF), D_MODEL),
        "ffn1_w_up": w(ks[3], (DEPTH, D_MODEL, D_FF), D_MODEL),
        "ffn1_w_down": w(ks[4], (DEPTH, D_FF, D_MODEL), D_FF),
        "mix_norm": gain(ks[5], (DEPTH, D_MODEL)),
        "ab_w_in": w(ks[6], (N_EVEN, D_MODEL, AB_IN_WIDTH), D_MODEL),
        "ab_v_norm": gain(ks[7], (N_EVEN, N_GATE_GROUPS, GATE_GROUP_DIM)),
        "ab_w_s": w(ks[8], (N_EVEN, N_GATE_GROUPS, CHUNK, CHUNK), CHUNK),
        "ab_b_s": gain(ks[9], (N_EVEN, N_GATE_GROUPS, CHUNK)),
        "ab_w_out": w(ks[10], (N_EVEN, AB_OUT_WIDTH, D_MODEL), AB_OUT_WIDTH),
        "attn_w_qkv": w(ks[11], (N_ODD, D_MODEL, QKV_WIDTH), D_MODEL),
        "attn_q_norm": gain(ks[12], (N_ODD, HEAD_DIM)),
        "attn_k_norm": gain(ks[13], (N_ODD, HEAD_DIM)),
        "attn_w_o": w(ks[14], (N_ODD, N_HEADS * HEAD_DIM, D_MODEL), N_HEADS * HEAD_DIM),
        "ffn2_norm": gain(ks[15], (DEPTH, D_MODEL)),
        "ffn2_w_gate": w(ks[16], (DEPTH, D_MODEL, D_FF), D_MODEL),
        "ffn2_w_up": w(ks[17], (DEPTH, D_MODEL, D_FF), D_MODEL),
        "ffn2_w_down": w(ks[18], (DEPTH, D_FF, D_MODEL), D_FF),
        "final_norm": gain(ks[19], (D_MODEL,)),
    }


def reference(x, ffn1_norm, ffn1_w_gate, ffn1_w_up, ffn1_w_down, mix_norm,
              ab_w_in, ab_v_norm, ab_w_s, ab_b_s, ab_w_out,
              attn_w_qkv, attn_q_norm, attn_k_norm, attn_w_o,
              ffn2_norm, ffn2_w_gate, ffn2_w_up, ffn2_w_down, final_norm):
    for layer in range(DEPTH):
        x = x + 0.5 * swiglu(rmsnorm(x, ffn1_norm[layer]), ffn1_w_gate[layer],
                             ffn1_w_up[layer], ffn1_w_down[layer])
        h = rmsnorm(x, mix_norm[layer])
        i = layer // 2
        if layer % 2 == 0:
            x = x + fourier_gating_mixer(h, ab_w_in[i], ab_v_norm[i], ab_w_s[i],
                                         ab_b_s[i], ab_w_out[i])
        else:
            x = x + axial_gqa(h, attn_w_qkv[i], attn_q_norm[i], attn_k_norm[i], attn_w_o[i])
        x = x + 0.5 * swiglu(rmsnorm(x, ffn2_norm[layer]), ffn2_w_gate[layer],
                             ffn2_w_up[layer], ffn2_w_down[layer])
    return rmsnorm(x, final_norm)
```

```python
import functools
import math

import numpy as np
import jax
import jax.numpy as jnp
from jax import lax
from jax.experimental import pallas as pl
from jax.experimental.pallas import tpu as pltpu

F32 = jnp.float32
BF16 = jnp.bfloat16

D_MODEL = 1024
D_FF = 2816
EPS = 1e-6
LANES = 128
N_GROUPS = 4
MIX_W = N_GROUPS * LANES
HEAD_DIM = 128
N_HEADS = 8
N_KV_HEADS = 2
KV_GROUP = N_HEADS // N_KV_HEADS
GRID_W = 64
ROPE_THETA = 10000.0
ROPE_FREQS = HEAD_DIM // 4
Q_SCALE = HEAD_DIM ** -0.5 * math.log2(math.e)
V7X_VMEM_BYTES = 64 * 1024 * 1024
VMEM_LIMIT = 56 * 1024 * 1024


def _params(n_axes):
    return pltpu.CompilerParams(
        dimension_semantics=("arbitrary",) * n_axes, vmem_limit_bytes=VMEM_LIMIT)


def _const_spec(shape):
    nd = len(shape)
    return pl.BlockSpec(shape, lambda *_: (0,) * nd, pipeline_mode=pl.Buffered(1))


def _rms(x, gain):
    return x * lax.rsqrt(jnp.mean(x * x, axis=-1, keepdims=True) + EPS) * gain


def _dot(a, b):
    return jnp.dot(a, b, preferred_element_type=F32)


def _sigmoid(x):
    return 1.0 / (1.0 + jnp.exp(-x))


def _gelu_tanh(x):
    c = math.sqrt(2.0 / math.pi)
    return x * (0.5 * (1.0 + jnp.tanh(c * (x + 0.044715 * (x * x * x)))))


def _swiglu_residual(x, gain, wg_ref, wu_ref, wd_ref):
    h = _rms(x, gain).astype(BF16)
    gate = _dot(h, wg_ref[...])
    up = _dot(h, wu_ref[...])
    a = (gate * _sigmoid(gate) * up).astype(BF16)
    return x + 0.5 * _dot(a, wd_ref[...])


def _ffn_kernel(*refs, n_pre, final_norm):
    x_ref = refs[0]
    pre = refs[1:1 + 2 * n_pre]
    g_ref, wg_ref, wu_ref, wd_ref = refs[1 + 2 * n_pre:5 + 2 * n_pre]
    rest = refs[5 + 2 * n_pre:]
    x = x_ref[...]
    for i in range(n_pre):
        x = x + _dot(pre[2 * i][...], pre[2 * i + 1][...])
    y = _swiglu_residual(x, g_ref[...], wg_ref, wu_ref, wd_ref)
    if final_norm:
        fg_ref, o_ref = rest
        y = _rms(y, fg_ref[...])
    else:
        (o_ref,) = rest
    o_ref[...] = y


def _ffn(x, pre, gain, wg, wu, wd, final_gain=None, *, tm=512, name="ffn"):
    S = x.shape[0]
    row = lambda w: pl.BlockSpec((tm, w), lambda i: (i, 0))
    args, specs = [x], [row(D_MODEL)]
    for a, w in pre:
        args += [a, w]
        specs += [row(a.shape[1]), _const_spec(w.shape)]
    args += [gain.reshape(1, D_MODEL), wg, wu, wd]
    specs += [_const_spec((1, D_MODEL)), _const_spec(wg.shape), _const_spec(wu.shape),
              _const_spec(wd.shape)]
    if final_gain is not None:
        args.append(final_gain.reshape(1, D_MODEL))
        specs.append(_const_spec((1, D_MODEL)))
    return pl.pallas_call(
        functools.partial(_ffn_kernel, n_pre=len(pre), final_norm=final_gain is not None),
        grid=(S // tm,),
        in_specs=specs,
        out_specs=row(D_MODEL),
        out_shape=jax.ShapeDtypeStruct((S, D_MODEL), F32),
        compiler_params=_params(1),
        name=name,
    )(*args)


def _ab_in_kernel(x_ref, g_ref, win_ref, cdft_ref, vn_ref, ws_ref, bs_ref, z_ref, gated_ref,
                  *, tm):
    h = _rms(x_ref[...], g_ref[...]).astype(BF16)
    z = _dot(h, win_ref[...])
    for g in range(N_GROUPS):
        fg = z[:, g * LANES:(g + 1) * LANES].astype(BF16)
        ab = _dot(fg, cdft_ref[...])
        z_ref[0, :, g * LANES:(g + 1) * LANES] = ab[:, :LANES].astype(BF16)
        z_ref[1, :, g * LANES:(g + 1) * LANES] = ab[:, LANES:].astype(BF16)
    uv = _gelu_tanh(z[:, MIX_W:])
    for g in range(N_GROUPS):
        u = uv[:, g * LANES:(g + 1) * LANES]
        v = uv[:, MIX_W + g * LANES:MIX_W + (g + 1) * LANES]
        vn = _rms(v, vn_ref[:, g * LANES:(g + 1) * LANES]).astype(BF16)
        for c in range(tm // LANES):
            rows = slice(c * LANES, (c + 1) * LANES)
            s = _dot(ws_ref[g], vn[rows]) + bs_ref[g]
            gated_ref[rows, g * LANES:(g + 1) * LANES] = (u[rows] * s).astype(BF16)


def _ab_in(x, gain, w_in, cdft, v_norm, w_s, b_full, *, tm=256):
    S = x.shape[0]
    return pl.pallas_call(
        functools.partial(_ab_in_kernel, tm=tm),
        grid=(S // tm,),
        in_specs=[
            pl.BlockSpec((tm, D_MODEL), lambda i: (i, 0)),
            _const_spec((1, D_MODEL)),
            _const_spec(w_in.shape),
            _const_spec(cdft.shape),
            _const_spec((1, MIX_W)),
            _const_spec(w_s.shape),
            _const_spec(b_full.shape),
        ],
        out_specs=[
            pl.BlockSpec((2, tm, MIX_W), lambda i: (0, i, 0)),
            pl.BlockSpec((tm, MIX_W), lambda i: (i, 0)),
        ],
        out_shape=[
            jax.ShapeDtypeStruct((2, S, MIX_W), BF16),
            jax.ShapeDtypeStruct((S, MIX_W), BF16),
        ],
        compiler_params=_params(1),
        name="ab_in",
    )(x, gain.reshape(1, D_MODEL), w_in, cdft, v_norm.reshape(1, MIX_W), w_s, b_full)


def _dft1_kernel(w_ref, z_ref, t_ref):
    t_ref[...] = _dot(w_ref[...], z_ref[...]).astype(BF16)


def _dft1(w1, zv, *, bn=4096):
    rows, cols = zv.shape
    return pl.pallas_call(
        _dft1_kernel,
        grid=(cols // bn,),
        in_specs=[_const_spec(w1.shape), pl.BlockSpec((rows, bn), lambda i: (0, i))],
        out_specs=pl.BlockSpec((rows, bn), lambda i: (0, i)),
        out_shape=jax.ShapeDtypeStruct((rows, cols), BF16),
        compiler_params=_params(1),
        name="dft_stage1",
    )(w1, zv)


def _dft2_kernel(g_ref, t_ref, o_ref, *, kb):
    for i in range(kb):
        tcat = jnp.concatenate([t_ref[0, i], t_ref[1, i]], axis=0)
        o_ref[:, i * MIX_W:(i + 1) * MIX_W] = _dot(g_ref[i], tcat).astype(BF16)


def _dft2(g2, t4, *, kb=8):
    _, n1, n2, w = t4.shape
    return pl.pallas_call(
        functools.partial(_dft2_kernel, kb=kb),
        grid=(n1 // kb,),
        in_specs=[
            pl.BlockSpec((kb, n2, 2 * n2), lambda i: (i, 0, 0)),
            pl.BlockSpec((2, kb, n2, w), lambda i: (0, i, 0, 0)),
        ],
        out_specs=pl.BlockSpec((n2, kb * w), lambda i: (0, i)),
        out_shape=jax.ShapeDtypeStruct((n2, n1 * w), BF16),
        compiler_params=_params(1),
        name="dft_stage2",
    )(g2, t4)


def _dft_tables(S):
    n1, n2 = S // LANES, LANES
    c = np.arange(LANES)
    ang = 2.0 * np.pi * np.outer(c, c) / LANES
    cdft = np.concatenate([np.cos(ang), np.sin(ang)], axis=1) / np.sqrt(LANES)
    a1 = 2.0 * np.pi * np.outer(np.arange(n1), np.arange(n1)) / n1
    c1, s1 = np.cos(a1), np.sin(a1)
    w1 = np.block([[c1, -s1], [-s1, -c1]]) / np.sqrt(n1)
    k = np.arange(n1)[:, None] + n1 * np.arange(n2)[None, :]
    a2 = 2.0 * np.pi * k[:, :, None] * np.arange(n2)[None, None, :] / S
    g2 = np.concatenate([np.cos(a2), np.sin(a2)], axis=-1) / np.sqrt(n2)
    to_bf16 = lambda a: jnp.asarray(a, dtype=F32).astype(BF16)
    return to_bf16(cdft), to_bf16(w1), to_bf16(g2)


def _qkv_kernel(x_ref, g_ref, w_ref, qn_ref, kn_ref, cos_ref, sin_ref, q_ref, k_ref, v_ref):
    h = _rms(x_ref[...], g_ref[...]).astype(BF16)
    qkv = _dot(h, w_ref[...])
    cosf = cos_ref[...]
    sinf = sin_ref[...]

    def head(col, gain, scale):
        y = _rms(qkv[:, col:col + HEAD_DIM], gain)
        y = y * cosf + pltpu.roll(y, HEAD_DIM // 2, 1) * sinf
        return (y * scale).astype(BF16)

    for i in range(N_HEADS):
        q_ref[:, i * HEAD_DIM:(i + 1) * HEAD_DIM] = head(i * HEAD_DIM, qn_ref[...], Q_SCALE)
    k0 = N_HEADS * HEAD_DIM
    v0 = k0 + N_KV_HEADS * HEAD_DIM
    for i in range(N_KV_HEADS):
        k_ref[:, i * HEAD_DIM:(i + 1) * HEAD_DIM] = head(k0 + i * HEAD_DIM, kn_ref[...], 1.0)
        vcol = 2 * i * HEAD_DIM
        v_ref[:, vcol:vcol + HEAD_DIM] = qkv[:, v0 + i * HEAD_DIM:v0 + (i + 1) * HEAD_DIM].astype(BF16)
        v_ref[:, vcol + HEAD_DIM:vcol + 2 * HEAD_DIM] = jnp.ones((x_ref.shape[0], HEAD_DIM), BF16)


def _qkv(x, gain, w, qn, kn, cosf, sinf, *, tm=256):
    S = x.shape[0]
    row = lambda wd: pl.BlockSpec((tm, wd), lambda i: (i, 0))
    return pl.pallas_call(
        _qkv_kernel,
        grid=(S // tm,),
        in_specs=[row(D_MODEL), _const_spec((1, D_MODEL)), _const_spec(w.shape),
                  _const_spec((1, HEAD_DIM)), _const_spec((1, HEAD_DIM)),
                  row(HEAD_DIM), row(HEAD_DIM)],
        out_specs=[row(N_HEADS * HEAD_DIM), row(N_KV_HEADS * HEAD_DIM),
                   row(2 * N_KV_HEADS * HEAD_DIM)],
        out_shape=[
            jax.ShapeDtypeStruct((S, N_HEADS * HEAD_DIM), BF16),
            jax.ShapeDtypeStruct((S, N_KV_HEADS * HEAD_DIM), BF16),
            jax.ShapeDtypeStruct((S, 2 * N_KV_HEADS * HEAD_DIM), BF16),
        ],
        compiler_params=_params(1),
        name="qkv_rope",
    )(x, gain.reshape(1, D_MODEL), w, qn.reshape(1, HEAD_DIM), kn.reshape(1, HEAD_DIM),
      cosf, sinf)


def _flash_kernel(q_ref, k_ref, v_ref, o_ref, m_sc, acc_sc, *, tq, tk):
    q = jnp.concatenate(
        [q_ref[:, g * HEAD_DIM:(g + 1) * HEAD_DIM] for g in range(KV_GROUP)], axis=0)
    m_sc[...] = jnp.full(m_sc.shape, -1e30, F32)
    acc_sc[...] = jnp.zeros(acc_sc.shape, F32)

    def body(j, carry):
        off = pl.multiple_of(j * tk, tk)
        kc = k_ref[pl.ds(off, tk), :]
        vc = v_ref[pl.ds(off, tk), :]
        s = lax.dot_general(q, kc, (((1,), (1,)), ((), ())), preferred_element_type=F32)
        m_old = m_sc[...]
        m_new = jnp.maximum(m_old, jnp.max(s, axis=-1, keepdims=True))
        alpha = jnp.exp2(m_old - m_new)
        p = jnp.exp2(s - jnp.tile(m_new, (1, tk // LANES)))
        acc_sc[...] = acc_sc[...] * jnp.tile(alpha, (1, 2)) + _dot(p.astype(BF16), vc)
        m_sc[...] = m_new
        return carry

    lax.fori_loop(0, k_ref.shape[0] // tk, body, 0)
    acc = acc_sc[...]
    o = acc[:, :HEAD_DIM] / acc[:, HEAD_DIM:]
    for g in range(KV_GROUP):
        o_ref[:, g * HEAD_DIM:(g + 1) * HEAD_DIM] = o[g * tq:(g + 1) * tq].astype(BF16)


def _flash(q, k, v, *, tq=128, tk=512):
    S = q.shape[0]
    gw = KV_GROUP * HEAD_DIM
    return pl.pallas_call(
        functools.partial(_flash_kernel, tq=tq, tk=tk),
        grid=(N_KV_HEADS, S // tq),
        in_specs=[
            pl.BlockSpec((tq, gw), lambda h, i: (i, h)),
            pl.BlockSpec((S, HEAD_DIM), lambda h, i: (0, h)),
            pl.BlockSpec((S, 2 * HEAD_DIM), lambda h, i: (0, h)),
        ],
        out_specs=pl.BlockSpec((tq, gw), lambda h, i: (i, h)),
        out_shape=jax.ShapeDtypeStruct((S, N_HEADS * HEAD_DIM), BF16),
        scratch_shapes=[
            pltpu.VMEM((KV_GROUP * tq, LANES), F32),
            pltpu.VMEM((KV_GROUP * tq, 2 * HEAD_DIM), F32),
        ],
        compiler_params=_params(2),
        name="gqa_flash",
    )(q, k, v)


def _rope_tables(S):
    rows = S // GRID_W
    row_idx = jnp.broadcast_to(jnp.arange(rows)[:, None], (rows, GRID_W)).reshape(S)
    col_idx = jnp.broadcast_to(jnp.arange(GRID_W)[None, :], (rows, GRID_W)).reshape(S)
    inv_freq = ROPE_THETA ** (-jnp.arange(ROPE_FREQS, dtype=F32) / ROPE_FREQS)
    ang = jnp.concatenate([row_idx.astype(F32)[:, None] * inv_freq[None, :],
                           col_idx.astype(F32)[:, None] * inv_freq[None, :]], axis=-1)
    cos, sin = jnp.cos(ang), jnp.sin(ang)
    return jnp.concatenate([cos, cos], axis=-1), jnp.concatenate([-sin, sin], axis=-1)


def kernel(x, ffn1_norm, ffn1_w_gate, ffn1_w_up, ffn1_w_down, mix_norm, ab_w_in, ab_v_norm, ab_w_s, ab_b_s, ab_w_out, attn_w_qkv, attn_q_norm, attn_k_norm, attn_w_o, ffn2_norm, ffn2_w_gate, ffn2_w_up, ffn2_w_down, final_norm):
    B, S, _ = x.shape
    assert B == 1 and S % 2048 == 0
    bf = lambda w: w.astype(BF16)
    xs = x.reshape(S, D_MODEL)

    xs = _ffn(xs, [], ffn1_norm[0], bf(ffn1_w_gate[0]), bf(ffn1_w_up[0]), bf(ffn1_w_down[0]),
              name="ffn1_l0")
    cdft, w1, g2 = _dft_tables(S)
    n1 = S // LANES
    b_full = jnp.broadcast_to(ab_b_s[0][:, :, None], (N_GROUPS, LANES, LANES))
    z, gated = _ab_in(xs, mix_norm[0], bf(ab_w_in[0]), cdft, ab_v_norm[0], bf(ab_w_s[0]), b_full)
    t = _dft1(w1, z.reshape(2 * n1, LANES * MIX_W))
    f = _dft2(g2, t.reshape(2, n1, LANES, MIX_W)).reshape(S, MIX_W)
    w_out = bf(ab_w_out[0])
    xs = _ffn(xs, [(f, w_out[:MIX_W]), (gated, w_out[MIX_W:])], ffn2_norm[0],
              bf(ffn2_w_gate[0]), bf(ffn2_w_up[0]), bf(ffn2_w_down[0]), name="mixout_ffn2_l0")

    xs = _ffn(xs, [], ffn1_norm[1], bf(ffn1_w_gate[1]), bf(ffn1_w_up[1]), bf(ffn1_w_down[1]),
              name="ffn1_l1")
    half = np.concatenate([np.arange(0, HEAD_DIM, 2), np.arange(1, HEAD_DIM, 2)])
    n_rot = N_HEADS + N_KV_HEADS
    cols = np.concatenate([(h * HEAD_DIM + half) for h in range(n_rot)]
                          + [np.arange(n_rot * HEAD_DIM, (n_rot + N_KV_HEADS) * HEAD_DIM)])
    w_qkv = bf(attn_w_qkv[0][:, cols])
    cosf, sinf = _rope_tables(S)
    q, k, v = _qkv(xs, mix_norm[1], w_qkv, attn_q_norm[0][half], attn_k_norm[0][half],
                   cosf, sinf)
    o = _flash(q, k, v)
    xs = _ffn(xs, [(o, bf(attn_w_o[0]))], ffn2_norm[1], bf(ffn2_w_gate[1]), bf(ffn2_w_up[1]),
              bf(ffn2_w_down[1]), final_norm, name="attnout_ffn2_final")
    return xs.reshape(B, S, D_MODEL)
```

```python
import functools
import math

import numpy as np
import jax
import jax.numpy as jnp
from jax import lax
from jax.experimental import pallas as pl
from jax.experimental.pallas import tpu as pltpu

F32 = jnp.float32
BF16 = jnp.bfloat16

D_MODEL = 1024
D_FF = 2816
EPS = 1e-6
LANES = 128
N_GROUPS = 4
MIX_W = N_GROUPS * LANES
HEAD_DIM = 128
N_HEADS = 8
N_KV_HEADS = 2
KV_GROUP = N_HEADS // N_KV_HEADS
GRID_W = 64
ROPE_THETA = 10000.0
ROPE_FREQS = HEAD_DIM // 4
Q_SCALE = HEAD_DIM ** -0.5 * math.log2(math.e)
V7X_VMEM_BYTES = 64 * 1024 * 1024
VMEM_LIMIT = 56 * 1024 * 1024


def _params(n_axes):
    return pltpu.CompilerParams(
        dimension_semantics=("arbitrary",) * n_axes, vmem_limit_bytes=VMEM_LIMIT)


def _const_spec(shape):
    nd = len(shape)
    return pl.BlockSpec(shape, lambda *_: (0,) * nd, pipeline_mode=pl.Buffered(1))


def _rms(x, gain):
    return x * lax.rsqrt(jnp.mean(x * x, axis=-1, keepdims=True) + EPS) * gain


def _dot(a, b):
    return jnp.dot(a, b, preferred_element_type=F32)


def _sigmoid(x):
    return 1.0 / (1.0 + jnp.exp(-x))


def _gelu_tanh(x):
    c = math.sqrt(2.0 / math.pi)
    return x * (0.5 * (1.0 + jnp.tanh(c * (x + 0.044715 * (x * x * x)))))


def _swiglu_residual(x, gain, wg_ref, wu_ref, wd_ref):
    h = _rms(x, gain).astype(BF16)
    gate = _dot(h, wg_ref[...])
    up = _dot(h, wu_ref[...])
    a = (gate * _sigmoid(gate) * up).astype(BF16)
    return x + 0.5 * _dot(a, wd_ref[...])


def _ffn_kernel(*refs, n_pre, final_norm):
    x_ref = refs[0]
    pre = refs[1:1 + 2 * n_pre]
    g_ref, wg_ref, wu_ref, wd_ref = refs[1 + 2 * n_pre:5 + 2 * n_pre]
    rest = refs[5 + 2 * n_pre:]
    x = x_ref[...]
    for i in range(n_pre):
        x = x + _dot(pre[2 * i][...], pre[2 * i + 1][...])
    y = _swiglu_residual(x, g_ref[...], wg_ref, wu_ref, wd_ref)
    if final_norm:
        fg_ref, o_ref = rest
        y = _rms(y, fg_ref[...])
    else:
        (o_ref,) = rest
    o_ref[...] = y


def _ffn(x, pre, gain, wg, wu, wd, final_gain=None, *, tm=512, name="ffn"):
    S = x.shape[0]
    row = lambda w: pl.BlockSpec((tm, w), lambda i: (i, 0))
    args, specs = [x], [row(D_MODEL)]
    for a, w in pre:
        args += [a, w]
        specs += [row(a.shape[1]), _const_spec(w.shape)]
    args += [gain.reshape(1, D_MODEL), wg, wu, wd]
    specs += [_const_spec((1, D_MODEL)), _const_spec(wg.shape), _const_spec(wu.shape),
              _const_spec(wd.shape)]
    if final_gain is not None:
        args.append(final_gain.reshape(1, D_MODEL))
        specs.append(_const_spec((1, D_MODEL)))
    return pl.pallas_call(
        functools.partial(_ffn_kernel, n_pre=len(pre), final_norm=final_gain is not None),
        grid=(S // tm,),
        in_specs=specs,
        out_specs=row(D_MODEL),
        out_shape=jax.ShapeDtypeStruct((S, D_MODEL), F32),
        compiler_params=_params(1),
        name=name,
    )(*args)


def _ab_in_kernel(x_ref, g_ref, win_ref, cdft_ref, vn_ref, ws_ref, bs_ref, z_ref, gated_ref,
                  *, tm):
    h = _rms(x_ref[...], g_ref[...]).astype(BF16)
    z = _dot(h, win_ref[...])
    for g in range(N_GROUPS):
        fg = z[:, g * LANES:(g + 1) * LANES].astype(BF16)
        ab = _dot(fg, cdft_ref[...])
        z_ref[0, :, g * LANES:(g + 1) * LANES] = ab[:, :LANES].astype(BF16)
        z_ref[1, :, g * LANES:(g + 1) * LANES] = ab[:, LANES:].astype(BF16)
    uv = _gelu_tanh(z[:, MIX_W:])
    for g in range(N_GROUPS):
        u = uv[:, g * LANES:(g + 1) * LANES]
        v = uv[:, MIX_W + g * LANES:MIX_W + (g + 1) * LANES]
        vn = _rms(v, vn_ref[:, g * LANES:(g + 1) * LANES]).astype(BF16)
        for c in range(tm // LANES):
            rows = slice(c * LANES, (c + 1) * LANES)
            s = _dot(ws_ref[g], vn[rows]) + bs_ref[g]
            gated_ref[rows, g * LANES:(g + 1) * LANES] = (u[rows] * s).astype(BF16)


def _ab_in(x, gain, w_in, cdft, v_norm, w_s, b_full, *, tm=256):
    S = x.shape[0]
    return pl.pallas_call(
        functools.partial(_ab_in_kernel, tm=tm),
        grid=(S // tm,),
        in_specs=[
            pl.BlockSpec((tm, D_MODEL), lambda i: (i, 0)),
            _const_spec((1, D_MODEL)),
            _const_spec(w_in.shape),
            _const_spec(cdft.shape),
            _const_spec((1, MIX_W)),
            _const_spec(w_s.shape),
            _const_spec(b_full.shape),
        ],
        out_specs=[
            pl.BlockSpec((2, tm, MIX_W), lambda i: (0, i, 0)),
            pl.BlockSpec((tm, MIX_W), lambda i: (i, 0)),
        ],
        out_shape=[
            jax.ShapeDtypeStruct((2, S, MIX_W), BF16),
            jax.ShapeDtypeStruct((S, MIX_W), BF16),
        ],
        compiler_params=_params(1),
        name="ab_in",
    )(x, gain.reshape(1, D_MODEL), w_in, cdft, v_norm.reshape(1, MIX_W), w_s, b_full)


def _dft1_kernel(w_ref, z_ref, t_ref):
    t_ref[...] = _dot(w_ref[...], z_ref[...]).astype(BF16)


def _dft1(w1, zv, *, bn=4096):
    rows, cols = zv.shape
    return pl.pallas_call(
        _dft1_kernel,
        grid=(cols // bn,),
        in_specs=[_const_spec(w1.shape), pl.BlockSpec((rows, bn), lambda i: (0, i))],
        out_specs=pl.BlockSpec((rows, bn), lambda i: (0, i)),
        out_shape=jax.ShapeDtypeStruct((rows, cols), BF16),
        compiler_params=_params(1),
        name="dft_stage1",
    )(w1, zv)


def _dft2_kernel(g_ref, t_ref, o_ref, *, kb):
    for i in range(kb):
        tcat = jnp.concatenate([t_ref[0, i], t_ref[1, i]], axis=0)
        o_ref[:, i * MIX_W:(i + 1) * MIX_W] = _dot(g_ref[i], tcat).astype(BF16)


def _dft2(g2, t4, *, kb=8):
    _, n1, n2, w = t4.shape
    return pl.pallas_call(
        functools.partial(_dft2_kernel, kb=kb),
        grid=(n1 // kb,),
        in_specs=[
            pl.BlockSpec((kb, n2, 2 * n2), lambda i: (i, 0, 0)),
            pl.BlockSpec((2, kb, n2, w), lambda i: (0, i, 0, 0)),
        ],
        out_specs=pl.BlockSpec((n2, kb * w), lambda i: (0, i)),
        out_shape=jax.ShapeDtypeStruct((n2, n1 * w), BF16),
        compiler_params=_params(1),
        name="dft_stage2",
    )(g2, t4)


def _dft_tables(S):
    n1, n2 = S // LANES, LANES
    c = np.arange(LANES)
    ang = 2.0 * np.pi * np.outer(c, c) / LANES
    cdft = np.concatenate([np.cos(ang), np.sin(ang)], axis=1) / np.sqrt(LANES)
    a1 = 2.0 * np.pi * np.outer(np.arange(n1), np.arange(n1)) / n1
    c1, s1 = np.cos(a1), np.sin(a1)
    w1 = np.block([[c1, -s1], [-s1, -c1]]) / np.sqrt(n1)
    k = np.arange(n1)[:, None] + n1 * np.arange(n2)[None, :]
    a2 = 2.0 * np.pi * k[:, :, None] * np.arange(n2)[None, None, :] / S
    g2 = np.concatenate([np.cos(a2), np.sin(a2)], axis=-1) / np.sqrt(n2)
    to_bf16 = lambda a: jnp.asarray(a, dtype=F32).astype(BF16)
    return to_bf16(cdft), to_bf16(w1), to_bf16(g2)


def _qkv_kernel(x_ref, g_ref, w_ref, qn_ref, kn_ref, cos_ref, sin_ref, q_ref, k_ref, v_ref):
    h = _rms(x_ref[...], g_ref[...]).astype(BF16)
    qkv = _dot(h, w_ref[...])
    cosf = cos_ref[...]
    sinf = sin_ref[...]

    def head(col, gain, scale):
        y = _rms(qkv[:, col:col + HEAD_DIM], gain)
        y = y * cosf + pltpu.roll(y, HEAD_DIM // 2, 1) * sinf
        return (y * scale).astype(BF16)

    for i in range(N_HEADS):
        q_ref[:, i * HEAD_DIM:(i + 1) * HEAD_DIM] = head(i * HEAD_DIM, qn_ref[...], Q_SCALE)
    k0 = N_HEADS * HEAD_DIM
    v0 = k0 + N_KV_HEADS * HEAD_DIM
    for i in range(N_KV_HEADS):
        k_ref[:, i * HEAD_DIM:(i + 1) * HEAD_DIM] = head(k0 + i * HEAD_DIM, kn_ref[...], 1.0)
        vcol = 2 * i * HEAD_DIM
        v_ref[:, vcol:vcol + HEAD_DIM] = qkv[:, v0 + i * HEAD_DIM:v0 + (i + 1) * HEAD_DIM].astype(BF16)
        v_ref[:, vcol + HEAD_DIM:vcol + 2 * HEAD_DIM] = jnp.ones((x_ref.shape[0], HEAD_DIM), BF16)


def _qkv(x, gain, w, qn, kn, cosf, sinf, *, tm=256):
    S = x.shape[0]
    row = lambda wd: pl.BlockSpec((tm, wd), lambda i: (i, 0))
    return pl.pallas_call(
        _qkv_kernel,
        grid=(S // tm,),
        in_specs=[row(D_MODEL), _const_spec((1, D_MODEL)), _const_spec(w.shape),
                  _const_spec((1, HEAD_DIM)), _const_spec((1, HEAD_DIM)),
                  row(HEAD_DIM), row(HEAD_DIM)],
        out_specs=[row(N_HEADS * HEAD_DIM), row(N_KV_HEADS * HEAD_DIM),
                   row(2 * N_KV_HEADS * HEAD_DIM)],
        out_shape=[
            jax.ShapeDtypeStruct((S, N_HEADS * HEAD_DIM), BF16),
            jax.ShapeDtypeStruct((S, N_KV_HEADS * HEAD_DIM), BF16),
            jax.ShapeDtypeStruct((S, 2 * N_KV_HEADS * HEAD_DIM), BF16),
        ],
        compiler_params=_params(1),
        name="qkv_rope",
    )(x, gain.reshape(1, D_MODEL), w, qn.reshape(1, HEAD_DIM), kn.reshape(1, HEAD_DIM),
      cosf, sinf)


def _flash_kernel(q_ref, k_ref, v_ref, o_ref, q_sc, s0_sc, s1_sc, p0_sc, p1_sc, al0_sc, al1_sc,
                  m_sc, acc_sc, *, tq, tk):
    nk = k_ref.shape[0] // tk
    s_sc, p_sc, al_sc = (s0_sc, s1_sc), (p0_sc, p1_sc), (al0_sc, al1_sc)
    for g in range(KV_GROUP):
        q_sc[g * tq:(g + 1) * tq, :] = q_ref[:, g * HEAD_DIM:(g + 1) * HEAD_DIM]
    m_sc[...] = jnp.full(m_sc.shape, -1e30, F32)
    acc_sc[...] = jnp.zeros(acc_sc.shape, F32)
    p_sc[1][...] = jnp.zeros(p_sc[1].shape, BF16)
    al_sc[1][...] = jnp.ones(al_sc[1].shape, F32)

    def scores(slot, j):
        off = pl.multiple_of(j * tk, tk)
        s_sc[slot][...] = lax.dot_general(q_sc[...], k_ref[pl.ds(off, tk), :],
                                          (((1,), (1,)), ((), ())), preferred_element_type=F32)

    def softmax(slot):
        s = s_sc[slot][...]
        m_old = m_sc[...]
        m_new = jnp.maximum(m_old, jnp.max(s, axis=-1, keepdims=True))
        al_sc[slot][...] = jnp.exp2(m_old - m_new)
        p_sc[slot][...] = jnp.exp2(s - jnp.tile(m_new, (1, tk // LANES))).astype(BF16)
        m_sc[...] = m_new

    def pv(slot, j):
        off = pl.multiple_of(j * tk, tk)
        acc_sc[...] = (acc_sc[...] * jnp.tile(al_sc[slot][...], (1, 2))
                       + _dot(p_sc[slot][...], v_ref[pl.ds(off, tk), :]))

    scores(0, 0)

    def body(i, carry):
        a = 2 * i
        pv(1, jnp.maximum(a - 1, 0))
        scores(1, a + 1)
        softmax(0)
        pv(0, a)
        scores(0, jnp.minimum(a + 2, nk - 1))
        softmax(1)
        return carry

    lax.fori_loop(0, nk // 2, body, 0)
    pv(1, nk - 1)
    acc = acc_sc[...]
    o = acc[:, :HEAD_DIM] / acc[:, HEAD_DIM:]
    for g in range(KV_GROUP):
        o_ref[:, g * HEAD_DIM:(g + 1) * HEAD_DIM] = o[g * tq:(g + 1) * tq].astype(BF16)


def _flash(q, k, v, *, tq=128, tk=512):
    S = q.shape[0]
    gw = KV_GROUP * HEAD_DIM
    mq = KV_GROUP * tq
    assert (S // tk) % 2 == 0
    return pl.pallas_call(
        functools.partial(_flash_kernel, tq=tq, tk=tk),
        grid=(N_KV_HEADS, S // tq),
        in_specs=[
            pl.BlockSpec((tq, gw), lambda h, i: (i, h)),
            pl.BlockSpec((S, HEAD_DIM), lambda h, i: (0, h)),
            pl.BlockSpec((S, 2 * HEAD_DIM), lambda h, i: (0, h)),
        ],
        out_specs=pl.BlockSpec((tq, gw), lambda h, i: (i, h)),
        out_shape=jax.ShapeDtypeStruct((S, N_HEADS * HEAD_DIM), BF16),
        scratch_shapes=[
            pltpu.VMEM((mq, HEAD_DIM), BF16),
            pltpu.VMEM((mq, tk), F32), pltpu.VMEM((mq, tk), F32),
            pltpu.VMEM((mq, tk), BF16), pltpu.VMEM((mq, tk), BF16),
            pltpu.VMEM((mq, LANES), F32), pltpu.VMEM((mq, LANES), F32),
            pltpu.VMEM((mq, LANES), F32),
            pltpu.VMEM((mq, 2 * HEAD_DIM), F32),
        ],
        compiler_params=_params(2),
        name="gqa_flash",
    )(q, k, v)


def _rope_tables(S):
    rows = S // GRID_W
    row_idx = jnp.broadcast_to(jnp.arange(rows)[:, None], (rows, GRID_W)).reshape(S)
    col_idx = jnp.broadcast_to(jnp.arange(GRID_W)[None, :], (rows, GRID_W)).reshape(S)
    inv_freq = ROPE_THETA ** (-jnp.arange(ROPE_FREQS, dtype=F32) / ROPE_FREQS)
    ang = jnp.concatenate([row_idx.astype(F32)[:, None] * inv_freq[None, :],
                           col_idx.astype(F32)[:, None] * inv_freq[None, :]], axis=-1)
    cos, sin = jnp.cos(ang), jnp.sin(ang)
    return jnp.concatenate([cos, cos], axis=-1), jnp.concatenate([-sin, sin], axis=-1)


def kernel(x, ffn1_norm, ffn1_w_gate, ffn1_w_up, ffn1_w_down, mix_norm, ab_w_in, ab_v_norm, ab_w_s, ab_b_s, ab_w_out, attn_w_qkv, attn_q_norm, attn_k_norm, attn_w_o, ffn2_norm, ffn2_w_gate, ffn2_w_up, ffn2_w_down, final_norm):
    B, S, _ = x.shape
    assert B == 1 and S % 2048 == 0
    bf = lambda w: w.astype(BF16)
    xs = x.reshape(S, D_MODEL)

    xs = _ffn(xs, [], ffn1_norm[0], bf(ffn1_w_gate[0]), bf(ffn1_w_up[0]), bf(ffn1_w_down[0]),
              name="ffn1_l0")
    cdft, w1, g2 = _dft_tables(S)
    n1 = S // LANES
    b_full = jnp.broadcast_to(ab_b_s[0][:, :, None], (N_GROUPS, LANES, LANES))
    z, gated = _ab_in(xs, mix_norm[0], bf(ab_w_in[0]), cdft, ab_v_norm[0], bf(ab_w_s[0]), b_full)
    t = _dft1(w1, z.reshape(2 * n1, LANES * MIX_W))
    f = _dft2(g2, t.reshape(2, n1, LANES, MIX_W)).reshape(S, MIX_W)
    w_out = bf(ab_w_out[0])
    xs = _ffn(xs, [(f, w_out[:MIX_W]), (gated, w_out[MIX_W:])], ffn2_norm[0],
              bf(ffn2_w_gate[0]), bf(ffn2_w_up[0]), bf(ffn2_w_down[0]), name="mixout_ffn2_l0")

    xs = _ffn(xs, [], ffn1_norm[1], bf(ffn1_w_gate[1]), bf(ffn1_w_up[1]), bf(ffn1_w_down[1]),
              name="ffn1_l1")
    half = np.concatenate([np.arange(0, HEAD_DIM, 2), np.arange(1, HEAD_DIM, 2)])
    n_rot = N_HEADS + N_KV_HEADS
    cols = np.concatenate([(h * HEAD_DIM + half) for h in range(n_rot)]
                          + [np.arange(n_rot * HEAD_DIM, (n_rot + N_KV_HEADS) * HEAD_DIM)])
    w_qkv = bf(attn_w_qkv[0][:, cols])
    cosf, sinf = _rope_tables(S)
    q, k, v = _qkv(xs, mix_norm[1], w_qkv, attn_q_norm[0][half], attn_k_norm[0][half],
                   cosf, sinf)
    o = _flash(q, k, v)
    xs = _ffn(xs, [(o, bf(attn_w_o[0]))], ffn2_norm[1], bf(ffn2_w_gate[1]), bf(ffn2_w_up[1]),
              bf(ffn2_w_down[1]), final_norm, name="attnout_ffn2_final")
    return xs.reshape(B, S, D_MODEL)
```

```python
import functools
import math

import numpy as np
import jax
import jax.numpy as jnp
from jax import lax
from jax.experimental import pallas as pl
from jax.experimental.pallas import tpu as pltpu

F32 = jnp.float32
BF16 = jnp.bfloat16

D_MODEL = 1024
D_FF = 2816
EPS = 1e-6
LANES = 128
N_GROUPS = 4
MIX_W = N_GROUPS * LANES
HEAD_DIM = 128
N_HEADS = 8
N_KV_HEADS = 2
KV_GROUP = N_HEADS // N_KV_HEADS
GRID_W = 64
ROPE_THETA = 10000.0
ROPE_FREQS = HEAD_DIM // 4
Q_SCALE = HEAD_DIM ** -0.5 * math.log2(math.e)
V7X_VMEM_BYTES = 64 * 1024 * 1024
VMEM_LIMIT = 56 * 1024 * 1024
FLASH_TQ = 256
FLASH_TK = 512
FLASH_NB = 4


def _params(n_axes, flags=None):
    return pltpu.CompilerParams(
        dimension_semantics=("arbitrary",) * n_axes, vmem_limit_bytes=VMEM_LIMIT, flags=flags)


def _const_spec(shape):
    nd = len(shape)
    return pl.BlockSpec(shape, lambda *_: (0,) * nd, pipeline_mode=pl.Buffered(1))


def _rms(x, gain):
    return x * lax.rsqrt(jnp.mean(x * x, axis=-1, keepdims=True) + EPS) * gain


def _dot(a, b):
    return jnp.dot(a, b, preferred_element_type=F32)


def _sigmoid(x):
    return 1.0 / (1.0 + jnp.exp(-x))


def _gelu_tanh(x):
    c = math.sqrt(2.0 / math.pi)
    return x * (0.5 * (1.0 + jnp.tanh(c * (x + 0.044715 * (x * x * x)))))


def _swiglu_residual(x, gain, wg_ref, wu_ref, wd_ref):
    h = _rms(x, gain).astype(BF16)
    gate = _dot(h, wg_ref[...])
    up = _dot(h, wu_ref[...])
    a = (gate * _sigmoid(gate) * up).astype(BF16)
    return x + 0.5 * _dot(a, wd_ref[...])


def _ffn_kernel(*refs, n_pre, final_norm):
    x_ref = refs[0]
    pre = refs[1:1 + 2 * n_pre]
    g_ref, wg_ref, wu_ref, wd_ref = refs[1 + 2 * n_pre:5 + 2 * n_pre]
    rest = refs[5 + 2 * n_pre:]
    x = x_ref[...]
    for i in range(n_pre):
        x = x + _dot(pre[2 * i][...], pre[2 * i + 1][...])
    y = _swiglu_residual(x, g_ref[...], wg_ref, wu_ref, wd_ref)
    if final_norm:
        fg_ref, o_ref = rest
        y = _rms(y, fg_ref[...])
    else:
        (o_ref,) = rest
    o_ref[...] = y


def _ffn(x, pre, gain, wg, wu, wd, final_gain=None, *, tm=512, name="ffn"):
    S = x.shape[0]
    row = lambda w: pl.BlockSpec((tm, w), lambda i: (i, 0))
    args, specs = [x], [row(D_MODEL)]
    for a, w in pre:
        args += [a, w]
        specs += [row(a.shape[1]), _const_spec(w.shape)]
    args += [gain.reshape(1, D_MODEL), wg, wu, wd]
    specs += [_const_spec((1, D_MODEL)), _const_spec(wg.shape), _const_spec(wu.shape),
              _const_spec(wd.shape)]
    if final_gain is not None:
        args.append(final_gain.reshape(1, D_MODEL))
        specs.append(_const_spec((1, D_MODEL)))
    return pl.pallas_call(
        functools.partial(_ffn_kernel, n_pre=len(pre), final_norm=final_gain is not None),
        grid=(S // tm,),
        in_specs=specs,
        out_specs=row(D_MODEL),
        out_shape=jax.ShapeDtypeStruct((S, D_MODEL), F32),
        compiler_params=_params(1),
        name=name,
    )(*args)


def _ab_in_kernel(x_ref, g_ref, win_ref, cdft_ref, vn_ref, ws_ref, bs_ref, z_ref, gated_ref,
                  *, tm):
    h = _rms(x_ref[...], g_ref[...]).astype(BF16)
    z = _dot(h, win_ref[...])
    for g in range(N_GROUPS):
        fg = z[:, g * LANES:(g + 1) * LANES].astype(BF16)
        ab = _dot(fg, cdft_ref[...])
        z_ref[0, :, g * LANES:(g + 1) * LANES] = ab[:, :LANES].astype(BF16)
        z_ref[1, :, g * LANES:(g + 1) * LANES] = ab[:, LANES:].astype(BF16)
    uv = _gelu_tanh(z[:, MIX_W:])
    for g in range(N_GROUPS):
        u = uv[:, g * LANES:(g + 1) * LANES]
        v = uv[:, MIX_W + g * LANES:MIX_W + (g + 1) * LANES]
        vn = _rms(v, vn_ref[:, g * LANES:(g + 1) * LANES]).astype(BF16)
        for c in range(tm // LANES):
            rows = slice(c * LANES, (c + 1) * LANES)
            s = _dot(ws_ref[g], vn[rows]) + bs_ref[g]
            gated_ref[rows, g * LANES:(g + 1) * LANES] = (u[rows] * s).astype(BF16)


def _ab_in(x, gain, w_in, cdft, v_norm, w_s, b_full, *, tm=256):
    S = x.shape[0]
    return pl.pallas_call(
        functools.partial(_ab_in_kernel, tm=tm),
        grid=(S // tm,),
        in_specs=[
            pl.BlockSpec((tm, D_MODEL), lambda i: (i, 0)),
            _const_spec((1, D_MODEL)),
            _const_spec(w_in.shape),
            _const_spec(cdft.shape),
            _const_spec((1, MIX_W)),
            _const_spec(w_s.shape),
            _const_spec(b_full.shape),
        ],
        out_specs=[
            pl.BlockSpec((2, tm, MIX_W), lambda i: (0, i, 0)),
            pl.BlockSpec((tm, MIX_W), lambda i: (i, 0)),
        ],
        out_shape=[
            jax.ShapeDtypeStruct((2, S, MIX_W), BF16),
            jax.ShapeDtypeStruct((S, MIX_W), BF16),
        ],
        compiler_params=_params(1),
        name="ab_in",
    )(x, gain.reshape(1, D_MODEL), w_in, cdft, v_norm.reshape(1, MIX_W), w_s, b_full)


def _dft1_kernel(w_ref, z_ref, t_ref):
    t_ref[...] = _dot(w_ref[...], z_ref[...]).astype(BF16)


def _dft1(w1, zv, *, bn=4096):
    rows, cols = zv.shape
    return pl.pallas_call(
        _dft1_kernel,
        grid=(cols // bn,),
        in_specs=[_const_spec(w1.shape), pl.BlockSpec((rows, bn), lambda i: (0, i))],
        out_specs=pl.BlockSpec((rows, bn), lambda i: (0, i)),
        out_shape=jax.ShapeDtypeStruct((rows, cols), BF16),
        compiler_params=_params(1),
        name="dft_stage1",
    )(w1, zv)


def _dft2_kernel(g_ref, t_ref, o_ref, *, kb):
    for i in range(kb):
        tcat = jnp.concatenate([t_ref[0, i], t_ref[1, i]], axis=0)
        o_ref[:, i * MIX_W:(i + 1) * MIX_W] = _dot(g_ref[i], tcat).astype(BF16)


def _dft2(g2, t4, *, kb=8):
    _, n1, n2, w = t4.shape
    return pl.pallas_call(
        functools.partial(_dft2_kernel, kb=kb),
        grid=(n1 // kb,),
        in_specs=[
            pl.BlockSpec((kb, n2, 2 * n2), lambda i: (i, 0, 0)),
            pl.BlockSpec((2, kb, n2, w), lambda i: (0, i, 0, 0)),
        ],
        out_specs=pl.BlockSpec((n2, kb * w), lambda i: (0, i)),
        out_shape=jax.ShapeDtypeStruct((n2, n1 * w), BF16),
        compiler_params=_params(1),
        name="dft_stage2",
    )(g2, t4)


def _dft_tables(S):
    n1, n2 = S // LANES, LANES
    c = np.arange(LANES)
    ang = 2.0 * np.pi * np.outer(c, c) / LANES
    cdft = np.concatenate([np.cos(ang), np.sin(ang)], axis=1) / np.sqrt(LANES)
    a1 = 2.0 * np.pi * np.outer(np.arange(n1), np.arange(n1)) / n1
    c1, s1 = np.cos(a1), np.sin(a1)
    w1 = np.block([[c1, -s1], [-s1, -c1]]) / np.sqrt(n1)
    k = np.arange(n1)[:, None] + n1 * np.arange(n2)[None, :]
    a2 = 2.0 * np.pi * k[:, :, None] * np.arange(n2)[None, None, :] / S
    g2 = np.concatenate([np.cos(a2), np.sin(a2)], axis=-1) / np.sqrt(n2)
    to_bf16 = lambda a: jnp.asarray(a, dtype=F32).astype(BF16)
    return to_bf16(cdft), to_bf16(w1), to_bf16(g2)


def _qkv_kernel(x_ref, g_ref, w_ref, qn_ref, kn_ref, cos_ref, sin_ref, q_ref, kt_ref, v_ref):
    h = _rms(x_ref[...], g_ref[...]).astype(BF16)
    qkv = _dot(h, w_ref[...])
    cosf = cos_ref[...]
    sinf = sin_ref[...]

    def head(col, gain):
        y = _rms(qkv[:, col:col + HEAD_DIM], gain)
        return y * cosf + pltpu.roll(y, HEAD_DIM // 2, 1) * sinf

    for i in range(N_HEADS):
        q_ref[:, i * HEAD_DIM:(i + 1) * HEAD_DIM] = (
            head(i * HEAD_DIM, qn_ref[...]) * Q_SCALE).astype(BF16)
    k0 = N_HEADS * HEAD_DIM
    v0 = k0 + N_KV_HEADS * HEAD_DIM
    for i in range(N_KV_HEADS):
        kt_ref[i * HEAD_DIM:(i + 1) * HEAD_DIM, :] = head(k0 + i * HEAD_DIM, kn_ref[...]).T.astype(BF16)
        vcol = 2 * i * HEAD_DIM
        v_ref[:, vcol:vcol + HEAD_DIM] = qkv[:, v0 + i * HEAD_DIM:v0 + (i + 1) * HEAD_DIM].astype(BF16)
        v_ref[:, vcol + HEAD_DIM:vcol + 2 * HEAD_DIM] = jnp.ones((x_ref.shape[0], HEAD_DIM), BF16)


def _qkv(x, gain, w, qn, kn, cosf, sinf, *, tm=256):
    S = x.shape[0]
    row = lambda wd: pl.BlockSpec((tm, wd), lambda i: (i, 0))
    return pl.pallas_call(
        _qkv_kernel,
        grid=(S // tm,),
        in_specs=[row(D_MODEL), _const_spec((1, D_MODEL)), _const_spec(w.shape),
                  _const_spec((1, HEAD_DIM)), _const_spec((1, HEAD_DIM)),
                  row(HEAD_DIM), row(HEAD_DIM)],
        out_specs=[row(N_HEADS * HEAD_DIM),
                   pl.BlockSpec((N_KV_HEADS * HEAD_DIM, tm), lambda i: (0, i)),
                   row(2 * N_KV_HEADS * HEAD_DIM)],
        out_shape=[
            jax.ShapeDtypeStruct((S, N_HEADS * HEAD_DIM), BF16),
            jax.ShapeDtypeStruct((N_KV_HEADS * HEAD_DIM, S), BF16),
            jax.ShapeDtypeStruct((S, 2 * N_KV_HEADS * HEAD_DIM), BF16),
        ],
        compiler_params=_params(1),
        name="qkv_rope",
    )(x, gain.reshape(1, D_MODEL), w, qn.reshape(1, HEAD_DIM), kn.reshape(1, HEAD_DIM),
      cosf, sinf)


def _flash_kernel(q_ref, kt_ref, v_ref, o_ref, q_sc, s0_sc, s1_sc, p0_sc, p1_sc, al0_sc, al1_sc,
                  mx0_sc, mx1_sc, m_sc, acc_sc, *, tq, tk, nb):
    nk = kt_ref.shape[1] // tk
    s_sc, p_sc, al_sc = (s0_sc, s1_sc), (p0_sc, p1_sc), (al0_sc, al1_sc)
    mx_sc = (mx0_sc, mx1_sc)
    for g in range(KV_GROUP):
        q_sc[g * tq:(g + 1) * tq, :] = q_ref[:, g * HEAD_DIM:(g + 1) * HEAD_DIM]
    m_sc[...] = jnp.full(m_sc.shape, -1e30, F32)
    acc_sc[...] = jnp.zeros(acc_sc.shape, F32)
    p_sc[1][...] = jnp.zeros(p_sc[1].shape, BF16)
    al_sc[1][...] = jnp.ones(al_sc[1].shape, F32)

    def scores(slot, j):
        off = pl.multiple_of(j * tk, tk)
        s = _dot(q_sc[...], kt_ref[:, pl.ds(off, tk)])
        s_sc[slot][...] = s
        mx = s[:, :LANES]
        for t in range(1, tk // LANES):
            mx = jnp.maximum(mx, s[:, t * LANES:(t + 1) * LANES])
        mx_sc[slot][...] = mx

    def softmax(slot):
        s = s_sc[slot][...]
        m_old = m_sc[...]
        m_new = jnp.maximum(m_old, jnp.max(mx_sc[slot][...], axis=-1, keepdims=True))
        al_sc[slot][...] = jnp.exp2(m_old - m_new)
        p_sc[slot][...] = jnp.exp2(s - jnp.tile(m_new, (1, tk // LANES))).astype(BF16)
        m_sc[...] = m_new

    def pv(slot, j):
        off = pl.multiple_of(j * tk, tk)
        acc_sc[...] = (acc_sc[...] * jnp.tile(al_sc[slot][...], (1, 2))
                       + _dot(p_sc[slot][...], v_ref[pl.ds(off, tk), :]))

    scores(0, 0)

    def body(i, carry):
        a = nb * i
        for c in range(nb):
            cur, nxt = c % 2, 1 - c % 2
            pv(nxt, jnp.maximum(a - 1, 0) if c == 0 else a + c - 1)
            scores(nxt, jnp.minimum(a + nb, nk - 1) if c == nb - 1 else a + c + 1)
            softmax(cur)
        return carry

    lax.fori_loop(0, nk // nb, body, 0)
    pv(1, nk - 1)
    acc = acc_sc[...]
    o = acc[:, :HEAD_DIM] / acc[:, HEAD_DIM:]
    for g in range(KV_GROUP):
        o_ref[:, g * HEAD_DIM:(g + 1) * HEAD_DIM] = o[g * tq:(g + 1) * tq].astype(BF16)


def _flash(q, k, v, *, tq=FLASH_TQ, tk=FLASH_TK, nb=FLASH_NB):
    S = q.shape[0]
    gw = KV_GROUP * HEAD_DIM
    mq = KV_GROUP * tq
    assert nb % 2 == 0 and (S // tk) % nb == 0
    return pl.pallas_call(
        functools.partial(_flash_kernel, tq=tq, tk=tk, nb=nb),
        grid=(N_KV_HEADS, S // tq),
        in_specs=[
            pl.BlockSpec((tq, gw), lambda h, i: (i, h)),
            pl.BlockSpec((HEAD_DIM, S), lambda h, i: (h, 0)),
            pl.BlockSpec((S, 2 * HEAD_DIM), lambda h, i: (0, h)),
        ],
        out_specs=pl.BlockSpec((tq, gw), lambda h, i: (i, h)),
        out_shape=jax.ShapeDtypeStruct((S, N_HEADS * HEAD_DIM), BF16),
        scratch_shapes=[
            pltpu.VMEM((mq, HEAD_DIM), BF16),
            pltpu.VMEM((mq, tk), F32), pltpu.VMEM((mq, tk), F32),
            pltpu.VMEM((mq, tk), BF16), pltpu.VMEM((mq, tk), BF16),
            pltpu.VMEM((mq, LANES), F32), pltpu.VMEM((mq, LANES), F32),
            pltpu.VMEM((mq, LANES), F32), pltpu.VMEM((mq, LANES), F32),
            pltpu.VMEM((mq, LANES), F32),
            pltpu.VMEM((mq, 2 * HEAD_DIM), F32),
        ],
        compiler_params=_params(2),
        name="gqa_flash",
    )(q, k, v)


def _rope_tables(S):
    rows = S // GRID_W
    row_idx = jnp.broadcast_to(jnp.arange(rows)[:, None], (rows, GRID_W)).reshape(S)
    col_idx = jnp.broadcast_to(jnp.arange(GRID_W)[None, :], (rows, GRID_W)).reshape(S)
    inv_freq = ROPE_THETA ** (-jnp.arange(ROPE_FREQS, dtype=F32) / ROPE_FREQS)
    ang = jnp.concatenate([row_idx.astype(F32)[:, None] * inv_freq[None, :],
                           col_idx.astype(F32)[:, None] * inv_freq[None, :]], axis=-1)
    cos, sin = jnp.cos(ang), jnp.sin(ang)
    return jnp.concatenate([cos, cos], axis=-1), jnp.concatenate([-sin, sin], axis=-1)


def kernel(x, ffn1_norm, ffn1_w_gate, ffn1_w_up, ffn1_w_down, mix_norm, ab_w_in, ab_v_norm, ab_w_s, ab_b_s, ab_w_out, attn_w_qkv, attn_q_norm, attn_k_norm, attn_w_o, ffn2_norm, ffn2_w_gate, ffn2_w_up, ffn2_w_down, final_norm):
    B, S, _ = x.shape
    assert B == 1 and S % 2048 == 0
    bf = lambda w: w.astype(BF16)
    xs = x.reshape(S, D_MODEL)

    xs = _ffn(xs, [], ffn1_norm[0], bf(ffn1_w_gate[0]), bf(ffn1_w_up[0]), bf(ffn1_w_down[0]),
              name="ffn1_l0")
    cdft, w1, g2 = _dft_tables(S)
    n1 = S // LANES
    b_full = jnp.broadcast_to(ab_b_s[0][:, :, None], (N_GROUPS, LANES, LANES))
    z, gated = _ab_in(xs, mix_norm[0], bf(ab_w_in[0]), cdft, ab_v_norm[0], bf(ab_w_s[0]), b_full)
    t = _dft1(w1, z.reshape(2 * n1, LANES * MIX_W))
    f = _dft2(g2, t.reshape(2, n1, LANES, MIX_W)).reshape(S, MIX_W)
    w_out = bf(ab_w_out[0])
    xs = _ffn(xs, [(f, w_out[:MIX_W]), (gated, w_out[MIX_W:])], ffn2_norm[0],
              bf(ffn2_w_gate[0]), bf(ffn2_w_up[0]), bf(ffn2_w_down[0]), name="mixout_ffn2_l0")

    xs = _ffn(xs, [], ffn1_norm[1], bf(ffn1_w_gate[1]), bf(ffn1_w_up[1]), bf(ffn1_w_down[1]),
              name="ffn1_l1")
    half = np.concatenate([np.arange(0, HEAD_DIM, 2), np.arange(1, HEAD_DIM, 2)])
    n_rot = N_HEADS + N_KV_HEADS
    cols = np.concatenate([(h * HEAD_DIM + half) for h in range(n_rot)]
                          + [np.arange(n_rot * HEAD_DIM, (n_rot + N_KV_HEADS) * HEAD_DIM)])
    w_qkv = bf(attn_w_qkv[0][:, cols])
    cosf, sinf = _rope_tables(S)
    q, k, v = _qkv(xs, mix_norm[1], w_qkv, attn_q_norm[0][half], attn_k_norm[0][half],
                   cosf, sinf)
    o = _flash(q, k, v)
    xs = _ffn(xs, [(o, bf(attn_w_o[0]))], ffn2_norm[1], bf(ffn2_w_gate[1]), bf(ffn2_w_up[1]),
              bf(ffn2_w_down[1]), final_norm, name="attnout_ffn2_final")
    return xs.reshape(B, S, D_MODEL)
```

```python
import functools
import math

import numpy as np
import jax
import jax.numpy as jnp
from jax import lax
from jax.experimental import pallas as pl
from jax.experimental.pallas import tpu as pltpu

F32 = jnp.float32
BF16 = jnp.bfloat16

D_MODEL = 1024
D_FF = 2816
EPS = 1e-6
LANES = 128
N_GROUPS = 4
MIX_W = N_GROUPS * LANES
HEAD_DIM = 128
N_HEADS = 8
N_KV_HEADS = 2
KV_GROUP = N_HEADS // N_KV_HEADS
GRID_W = 64
ROPE_THETA = 10000.0
ROPE_FREQS = HEAD_DIM // 4
Q_SCALE = HEAD_DIM ** -0.5 * math.log2(math.e)
V7X_VMEM_BYTES = 64 * 1024 * 1024
VMEM_LIMIT = 56 * 1024 * 1024
FLASH_TQ = 256
FLASH_TK = 512
FLASH_NB = 8


def _params(n_axes, flags=None):
    return pltpu.CompilerParams(
        dimension_semantics=("arbitrary",) * n_axes, vmem_limit_bytes=VMEM_LIMIT, flags=flags)


def _const_spec(shape):
    nd = len(shape)
    return pl.BlockSpec(shape, lambda *_: (0,) * nd, pipeline_mode=pl.Buffered(1))


def _rms(x, gain):
    return x * lax.rsqrt(jnp.mean(x * x, axis=-1, keepdims=True) + EPS) * gain


def _dot(a, b):
    return jnp.dot(a, b, preferred_element_type=F32)


def _sigmoid(x):
    return 1.0 / (1.0 + jnp.exp(-x))


def _gelu_tanh(x):
    c = math.sqrt(2.0 / math.pi)
    return x * (0.5 * (1.0 + jnp.tanh(c * (x + 0.044715 * (x * x * x)))))


def _swiglu_residual(x, gain, wg_ref, wu_ref, wd_ref):
    h = _rms(x, gain).astype(BF16)
    gate = _dot(h, wg_ref[...])
    up = _dot(h, wu_ref[...])
    a = (gate * _sigmoid(gate) * up).astype(BF16)
    return x + 0.5 * _dot(a, wd_ref[...])


def _ffn_kernel(*refs, n_pre, final_norm):
    x_ref = refs[0]
    pre = refs[1:1 + 2 * n_pre]
    g_ref, wg_ref, wu_ref, wd_ref = refs[1 + 2 * n_pre:5 + 2 * n_pre]
    rest = refs[5 + 2 * n_pre:]
    x = x_ref[...]
    for i in range(n_pre):
        x = x + _dot(pre[2 * i][...], pre[2 * i + 1][...])
    y = _swiglu_residual(x, g_ref[...], wg_ref, wu_ref, wd_ref)
    if final_norm:
        fg_ref, o_ref = rest
        y = _rms(y, fg_ref[...])
    else:
        (o_ref,) = rest
    o_ref[...] = y


def _layer_spec(w, layer):
    return pl.BlockSpec((None,) + w.shape[1:], lambda *_: (layer, 0, 0),
                        pipeline_mode=pl.Buffered(1))


def _ffn(x, pre, gain, wg, wu, wd, layer, final_gain=None, *, tm=512, name="ffn"):
    S = x.shape[0]
    row = lambda w: pl.BlockSpec((tm, w), lambda i: (i, 0))
    args, specs = [x], [row(D_MODEL)]
    for a, w, blk in pre:
        args += [a, w]
        specs += [row(a.shape[1]),
                  pl.BlockSpec((a.shape[1], D_MODEL), lambda i, blk=blk: (blk, 0),
                               pipeline_mode=pl.Buffered(1))]
    args += [gain.reshape(1, D_MODEL), wg, wu, wd]
    specs += [_const_spec((1, D_MODEL)), _layer_spec(wg, layer), _layer_spec(wu, layer),
              _layer_spec(wd, layer)]
    if final_gain is not None:
        args.append(final_gain.reshape(1, D_MODEL))
        specs.append(_const_spec((1, D_MODEL)))
    return pl.pallas_call(
        functools.partial(_ffn_kernel, n_pre=len(pre), final_norm=final_gain is not None),
        grid=(S // tm,),
        in_specs=specs,
        out_specs=row(D_MODEL),
        out_shape=jax.ShapeDtypeStruct((S, D_MODEL), F32),
        compiler_params=_params(1),
        name=name,
    )(*args)


def _ab_in_kernel(x_ref, g_ref, win_ref, cdft_ref, vn_ref, ws_ref, bs_ref, z_ref, gated_ref,
                  *, tm):
    h = _rms(x_ref[...], g_ref[...]).astype(BF16)
    z = _dot(h, win_ref[...])
    for g in range(N_GROUPS):
        fg = z[:, g * LANES:(g + 1) * LANES].astype(BF16)
        ab = _dot(fg, cdft_ref[...])
        z_ref[0, :, g * LANES:(g + 1) * LANES] = ab[:, :LANES].astype(BF16)
        z_ref[1, :, g * LANES:(g + 1) * LANES] = ab[:, LANES:].astype(BF16)
    uv = _gelu_tanh(z[:, MIX_W:])
    for g in range(N_GROUPS):
        u = uv[:, g * LANES:(g + 1) * LANES]
        v = uv[:, MIX_W + g * LANES:MIX_W + (g + 1) * LANES]
        vn = _rms(v, vn_ref[:, g * LANES:(g + 1) * LANES]).astype(BF16)
        for c in range(tm // LANES):
            rows = slice(c * LANES, (c + 1) * LANES)
            s = _dot(ws_ref[g], vn[rows]) + bs_ref[g]
            gated_ref[rows, g * LANES:(g + 1) * LANES] = (u[rows] * s).astype(BF16)


def _ab_in(x, gain, w_in, cdft, v_norm, w_s, b_full, *, tm=256):
    S = x.shape[0]
    return pl.pallas_call(
        functools.partial(_ab_in_kernel, tm=tm),
        grid=(S // tm,),
        in_specs=[
            pl.BlockSpec((tm, D_MODEL), lambda i: (i, 0)),
            _const_spec((1, D_MODEL)),
            _const_spec(w_in.shape),
            _const_spec(cdft.shape),
            _const_spec((1, MIX_W)),
            _const_spec(w_s.shape),
            _const_spec(b_full.shape),
        ],
        out_specs=[
            pl.BlockSpec((2, tm, MIX_W), lambda i: (0, i, 0)),
            pl.BlockSpec((tm, MIX_W), lambda i: (i, 0)),
        ],
        out_shape=[
            jax.ShapeDtypeStruct((2, S, MIX_W), BF16),
            jax.ShapeDtypeStruct((S, MIX_W), BF16),
        ],
        compiler_params=_params(1),
        name="ab_in",
    )(x, gain.reshape(1, D_MODEL), w_in, cdft, v_norm.reshape(1, MIX_W), w_s, b_full)


def _dft1_kernel(w_ref, z_ref, t_ref):
    t_ref[...] = _dot(w_ref[...], z_ref[...]).astype(BF16)


def _dft1(w1, zv, *, bn=4096):
    rows, cols = zv.shape
    return pl.pallas_call(
        _dft1_kernel,
        grid=(cols // bn,),
        in_specs=[_const_spec(w1.shape), pl.BlockSpec((rows, bn), lambda i: (0, i))],
        out_specs=pl.BlockSpec((rows, bn), lambda i: (0, i)),
        out_shape=jax.ShapeDtypeStruct((rows, cols), BF16),
        compiler_params=_params(1),
        name="dft_stage1",
    )(w1, zv)


def _dft2_kernel(g_ref, t_ref, o_ref, *, kb):
    for i in range(kb):
        tcat = jnp.concatenate([t_ref[0, i], t_ref[1, i]], axis=0)
        o_ref[:, i * MIX_W:(i + 1) * MIX_W] = _dot(g_ref[i], tcat).astype(BF16)


def _dft2(g2, t4, *, kb=8):
    _, n1, n2, w = t4.shape
    return pl.pallas_call(
        functools.partial(_dft2_kernel, kb=kb),
        grid=(n1 // kb,),
        in_specs=[
            pl.BlockSpec((kb, n2, 2 * n2), lambda i: (i, 0, 0)),
            pl.BlockSpec((2, kb, n2, w), lambda i: (0, i, 0, 0)),
        ],
        out_specs=pl.BlockSpec((n2, kb * w), lambda i: (0, i)),
        out_shape=jax.ShapeDtypeStruct((n2, n1 * w), BF16),
        compiler_params=_params(1),
        name="dft_stage2",
    )(g2, t4)


def _dft_tables(S):
    n1, n2 = S // LANES, LANES
    c = np.arange(LANES)
    ang = 2.0 * np.pi * np.outer(c, c) / LANES
    cdft = np.concatenate([np.cos(ang), np.sin(ang)], axis=1) / np.sqrt(LANES)
    a1 = 2.0 * np.pi * np.outer(np.arange(n1), np.arange(n1)) / n1
    c1, s1 = np.cos(a1), np.sin(a1)
    w1 = np.block([[c1, -s1], [-s1, -c1]]) / np.sqrt(n1)
    k = np.arange(n1)[:, None] + n1 * np.arange(n2)[None, :]
    a2 = 2.0 * np.pi * k[:, :, None] * np.arange(n2)[None, None, :] / S
    g2 = np.concatenate([np.cos(a2), np.sin(a2)], axis=-1) / np.sqrt(n2)
    to_bf16 = lambda a: jnp.asarray(a, dtype=F32).astype(BF16)
    return to_bf16(cdft), to_bf16(w1), to_bf16(g2)


def _qkv_kernel(x_ref, g_ref, w_ref, qn_ref, kn_ref, cos_ref, sin_ref, q_ref, kt_ref, v_ref):
    h = _rms(x_ref[...], g_ref[...]).astype(BF16)
    qkv = _dot(h, w_ref[...])
    cosf = cos_ref[...]
    sinf = sin_ref[...]

    def head(col, gain):
        y = _rms(qkv[:, col:col + HEAD_DIM], gain)
        return y * cosf + pltpu.roll(y, HEAD_DIM // 2, 1) * sinf

    for i in range(N_HEADS):
        q_ref[:, i * HEAD_DIM:(i + 1) * HEAD_DIM] = (
            head(i * HEAD_DIM, qn_ref[...]) * Q_SCALE).astype(BF16)
    k0 = N_HEADS * HEAD_DIM
    v0 = k0 + N_KV_HEADS * HEAD_DIM
    for i in range(N_KV_HEADS):
        kt_ref[i * HEAD_DIM:(i + 1) * HEAD_DIM, :] = head(k0 + i * HEAD_DIM, kn_ref[...]).T.astype(BF16)
        vcol = 2 * i * HEAD_DIM
        v_ref[:, vcol:vcol + HEAD_DIM] = qkv[:, v0 + i * HEAD_DIM:v0 + (i + 1) * HEAD_DIM].astype(BF16)
        v_ref[:, vcol + HEAD_DIM:vcol + 2 * HEAD_DIM] = jnp.ones((x_ref.shape[0], HEAD_DIM), BF16)


def _qkv(x, gain, w, qn, kn, cosf, sinf, *, tm=256):
    S = x.shape[0]
    row = lambda wd: pl.BlockSpec((tm, wd), lambda i: (i, 0))
    return pl.pallas_call(
        _qkv_kernel,
        grid=(S // tm,),
        in_specs=[row(D_MODEL), _const_spec((1, D_MODEL)), _const_spec(w.shape),
                  _const_spec((1, HEAD_DIM)), _const_spec((1, HEAD_DIM)),
                  row(HEAD_DIM), row(HEAD_DIM)],
        out_specs=[row(N_HEADS * HEAD_DIM),
                   pl.BlockSpec((N_KV_HEADS * HEAD_DIM, tm), lambda i: (0, i)),
                   row(2 * N_KV_HEADS * HEAD_DIM)],
        out_shape=[
            jax.ShapeDtypeStruct((S, N_HEADS * HEAD_DIM), BF16),
            jax.ShapeDtypeStruct((N_KV_HEADS * HEAD_DIM, S), BF16),
            jax.ShapeDtypeStruct((S, 2 * N_KV_HEADS * HEAD_DIM), BF16),
        ],
        compiler_params=_params(1),
        name="qkv_rope",
    )(x, gain.reshape(1, D_MODEL), w, qn.reshape(1, HEAD_DIM), kn.reshape(1, HEAD_DIM),
      cosf, sinf)


def _flash_kernel(q_ref, kt_ref, v_ref, o_ref, q_sc, s_sc, mx_sc, p_sc, al_sc, m_sc, acc_sc,
                  *, tq, tk, nb):
    nk = kt_ref.shape[1] // tk
    for g in range(KV_GROUP):
        q_sc[g * tq:(g + 1) * tq, :] = q_ref[:, g * HEAD_DIM:(g + 1) * HEAD_DIM]

    def scores(j):
        off = pl.multiple_of(j * tk, tk)
        s = _dot(q_sc[...], kt_ref[:, pl.ds(off, tk)])
        mx = s[:, :LANES]
        for t in range(1, tk // LANES):
            mx = jnp.maximum(mx, s[:, t * LANES:(t + 1) * LANES])
        return s, mx

    def softmax(s, mx, m_old):
        m_new = jnp.maximum(m_old, jnp.max(mx, axis=-1, keepdims=True))
        alpha = jnp.exp2(m_old - m_new)
        p = jnp.exp2(s - jnp.tile(m_new, (1, tk // LANES))).astype(BF16)
        return p, alpha, m_new

    def pv(acc, p, alpha, j):
        off = pl.multiple_of(j * tk, tk)
        return acc * jnp.tile(alpha, (1, 2)) + _dot(p, v_ref[pl.ds(off, tk), :])

    s0, mx0 = scores(0)
    s_sc[...] = s0
    mx_sc[...] = mx0
    m_sc[...] = jnp.full(m_sc.shape, -1e30, F32)
    acc_sc[...] = jnp.zeros(acc_sc.shape, F32)
    p_sc[...] = jnp.zeros(p_sc.shape, BF16)
    al_sc[...] = jnp.ones(al_sc.shape, F32)

    def body(i, carry):
        a = nb * i
        s, mx, p, alpha = s_sc[...], mx_sc[...], p_sc[...], al_sc[...]
        m, acc = m_sc[...], acc_sc[...]
        for c in range(nb):
            j = a + c
            acc = pv(acc, p, alpha, jnp.maximum(j - 1, 0) if c == 0 else j - 1)
            s_next, mx_next = scores(jnp.minimum(j + 1, nk - 1) if c == nb - 1 else j + 1)
            p, alpha, m = softmax(s, mx, m)
            s, mx = s_next, mx_next
        s_sc[...], mx_sc[...], p_sc[...], al_sc[...] = s, mx, p, alpha
        m_sc[...], acc_sc[...] = m, acc
        return carry

    lax.fori_loop(0, nk // nb, body, 0)
    acc = pv(acc_sc[...], p_sc[...], al_sc[...], nk - 1)
    o = acc[:, :HEAD_DIM] / acc[:, HEAD_DIM:]
    for g in range(KV_GROUP):
        o_ref[:, g * HEAD_DIM:(g + 1) * HEAD_DIM] = o[g * tq:(g + 1) * tq].astype(BF16)


def _flash(q, k, v, *, tq=FLASH_TQ, tk=FLASH_TK, nb=FLASH_NB):
    S = q.shape[0]
    gw = KV_GROUP * HEAD_DIM
    mq = KV_GROUP * tq
    nb = min(nb, S // tk)
    assert (S // tk) % nb == 0
    return pl.pallas_call(
        functools.partial(_flash_kernel, tq=tq, tk=tk, nb=nb),
        grid=(N_KV_HEADS, S // tq),
        in_specs=[
            pl.BlockSpec((tq, gw), lambda h, i: (i, h)),
            pl.BlockSpec((HEAD_DIM, S), lambda h, i: (h, 0)),
            pl.BlockSpec((S, 2 * HEAD_DIM), lambda h, i: (0, h)),
        ],
        out_specs=pl.BlockSpec((tq, gw), lambda h, i: (i, h)),
        out_shape=jax.ShapeDtypeStruct((S, N_HEADS * HEAD_DIM), BF16),
        scratch_shapes=[
            pltpu.VMEM((mq, HEAD_DIM), BF16),
            pltpu.VMEM((mq, tk), F32),
            pltpu.VMEM((mq, LANES), F32),
            pltpu.VMEM((mq, tk), BF16),
            pltpu.VMEM((mq, LANES), F32),
            pltpu.VMEM((mq, LANES), F32),
            pltpu.VMEM((mq, 2 * HEAD_DIM), F32),
        ],
        compiler_params=_params(2),
        name="gqa_flash",
    )(q, k, v)


def _rope_tables(S):
    rows = S // GRID_W
    inv_freq = ROPE_THETA ** (-jnp.arange(ROPE_FREQS, dtype=F32) / ROPE_FREQS)

    def axis_tables(n):
        ang = jnp.arange(n).astype(F32)[:, None] * inv_freq[None, :]
        return jnp.cos(ang), jnp.sin(ang)

    def per_token(row_t, col_t):
        return jnp.concatenate([jnp.repeat(row_t, GRID_W, axis=0), jnp.tile(col_t, (rows, 1))],
                               axis=-1)

    (cos_r, sin_r), (cos_c, sin_c) = axis_tables(rows), axis_tables(GRID_W)
    cos, sin = per_token(cos_r, cos_c), per_token(sin_r, sin_c)
    return jnp.concatenate([cos, cos], axis=-1), jnp.concatenate([-sin, sin], axis=-1)


def kernel(x, ffn1_norm, ffn1_w_gate, ffn1_w_up, ffn1_w_down, mix_norm, ab_w_in, ab_v_norm, ab_w_s, ab_b_s, ab_w_out, attn_w_qkv, attn_q_norm, attn_k_norm, attn_w_o, ffn2_norm, ffn2_w_gate, ffn2_w_up, ffn2_w_down, final_norm):
    B, S, _ = x.shape
    assert B == 1 and S % 2048 == 0
    bf = lambda w: w.astype(BF16)
    xs = x.reshape(S, D_MODEL)

    ffn1_w = (bf(ffn1_w_gate), bf(ffn1_w_up), bf(ffn1_w_down))
    ffn2_w = (bf(ffn2_w_gate), bf(ffn2_w_up), bf(ffn2_w_down))
    xs = _ffn(xs, [], ffn1_norm[0], *ffn1_w, 0, name="ffn1_l0")
    cdft, w1, g2 = _dft_tables(S)
    n1 = S // LANES
    b_full = jnp.broadcast_to(ab_b_s[0][:, :, None], (N_GROUPS, LANES, LANES))
    z, gated = _ab_in(xs, mix_norm[0], bf(ab_w_in[0]), cdft, ab_v_norm[0], bf(ab_w_s[0]), b_full)
    t = _dft1(w1, z.reshape(2 * n1, LANES * MIX_W))
    f = _dft2(g2, t.reshape(2, n1, LANES, MIX_W)).reshape(S, MIX_W)
    w_out = bf(ab_w_out[0])
    xs = _ffn(xs, [(f, w_out, 0), (gated, w_out, 1)], ffn2_norm[0], *ffn2_w, 0,
              name="mixout_ffn2_l0")

    xs = _ffn(xs, [], ffn1_norm[1], *ffn1_w, 1, name="ffn1_l1")
    half = np.concatenate([np.arange(0, HEAD_DIM, 2), np.arange(1, HEAD_DIM, 2)])
    n_rot = N_HEADS + N_KV_HEADS
    cols = np.concatenate([(h * HEAD_DIM + half) for h in range(n_rot)]
                          + [np.arange(n_rot * HEAD_DIM, (n_rot + N_KV_HEADS) * HEAD_DIM)])
    w_qkv = bf(attn_w_qkv[0][:, cols])
    cosf, sinf = _rope_tables(S)
    q, k, v = _qkv(xs, mix_norm[1], w_qkv, attn_q_norm[0][half], attn_k_norm[0][half],
                   cosf, sinf)
    o = _flash(q, k, v)
    xs = _ffn(xs, [(o, bf(attn_w_o[0]), 0)], ffn2_norm[1], *ffn2_w, 1, final_norm,
              name="attnout_ffn2_final")
    return xs.reshape(B, S, D_MODEL)
```

```python
import functools
import math

import numpy as np
import jax
import jax.numpy as jnp
from jax import lax
from jax.experimental import pallas as pl
from jax.experimental.pallas import tpu as pltpu

F32 = jnp.float32
BF16 = jnp.bfloat16

D_MODEL = 1024
D_FF = 2816
EPS = 1e-6
LANES = 128
N_GROUPS = 4
MIX_W = N_GROUPS * LANES
HEAD_DIM = 128
N_HEADS = 8
N_KV_HEADS = 2
KV_GROUP = N_HEADS // N_KV_HEADS
GRID_W = 64
ROPE_THETA = 10000.0
ROPE_FREQS = HEAD_DIM // 4
Q_SCALE = HEAD_DIM ** -0.5 * math.log2(math.e)
VMEM_LIMIT = 56 * 1024 * 1024
FFN_TM = 512
FLASH_TQ = 256
FLASH_TILES = 4
FLASH_TK = 512
FLASH_NB = 16


def _params(n_axes):
    return pltpu.CompilerParams(
        dimension_semantics=("arbitrary",) * n_axes, vmem_limit_bytes=VMEM_LIMIT)


def _const_spec(shape):
    nd = len(shape)
    return pl.BlockSpec(shape, lambda *_: (0,) * nd, pipeline_mode=pl.Buffered(1))


def _layer_spec(w, layer):
    return pl.BlockSpec((None,) + w.shape[1:], lambda *_: (layer, 0, 0),
                        pipeline_mode=pl.Buffered(1))


def _rms(x, gain):
    return x * lax.rsqrt(jnp.mean(x * x, axis=-1, keepdims=True) + EPS) * gain


def _dot(a, b):
    return jnp.dot(a, b, preferred_element_type=F32)


def _sigmoid(x):
    return 1.0 / (1.0 + jnp.exp(-x))


def _gelu_tanh(x):
    c = math.sqrt(2.0 / math.pi)
    return x * (0.5 * (1.0 + jnp.tanh(c * (x + 0.044715 * (x * x * x)))))


def _swiglu_residual(x, gain, wg_ref, wu_ref, wd_ref):
    h = _rms(x, gain).astype(BF16)
    gate = _dot(h, wg_ref[...])
    up = _dot(h, wu_ref[...])
    a = (gate * _sigmoid(gate) * up).astype(BF16)
    return x + 0.5 * _dot(a, wd_ref[...])


def _ffn_kernel(*refs, n_pre, final_norm, post_fn, n_post_in):
    n_in = 5 + 2 * n_pre + int(final_norm) + n_post_in
    x_ref = refs[0]
    pre = refs[1:1 + 2 * n_pre]
    g_ref, wg_ref, wu_ref, wd_ref = refs[1 + 2 * n_pre:5 + 2 * n_pre]
    o_ref = refs[n_in]
    x = x_ref[...]
    for i in range(n_pre):
        x = x + _dot(pre[2 * i][...], pre[2 * i + 1][...])
    y = _swiglu_residual(x, g_ref[...], wg_ref, wu_ref, wd_ref)
    if final_norm:
        y = _rms(y, refs[5 + 2 * n_pre][...])
    o_ref[...] = y
    if post_fn is not None:
        post_fn(y, *refs[n_in - n_post_in:n_in], *refs[n_in + 1:])


def _ffn(x, pre, gain, wg, wu, wd, layer, final_gain=None, post=None, *, tm=FFN_TM, name="ffn"):
    S = x.shape[0]
    row = lambda w: pl.BlockSpec((tm, w), lambda i: (i, 0))
    args, specs = [x], [row(D_MODEL)]
    for a, w, blk in pre:
        args += [a, w]
        specs += [row(a.shape[1]),
                  pl.BlockSpec((a.shape[1], D_MODEL), lambda i, blk=blk: (blk, 0),
                               pipeline_mode=pl.Buffered(1))]
    args += [gain.reshape(1, D_MODEL), wg, wu, wd]
    specs += [_const_spec((1, D_MODEL)), _layer_spec(wg, layer), _layer_spec(wu, layer),
              _layer_spec(wd, layer)]
    if final_gain is not None:
        args.append(final_gain.reshape(1, D_MODEL))
        specs.append(_const_spec((1, D_MODEL)))
    post_fn, post_args, post_in, post_out, post_shapes = post or (None, [], [], [], [])
    out = pl.pallas_call(
        functools.partial(_ffn_kernel, n_pre=len(pre), final_norm=final_gain is not None,
                          post_fn=post_fn, n_post_in=len(post_args)),
        grid=(S // tm,),
        in_specs=specs + post_in,
        out_specs=[row(D_MODEL)] + post_out,
        out_shape=[jax.ShapeDtypeStruct((S, D_MODEL), F32)] + post_shapes,
        compiler_params=_params(1),
        name=name,
    )(*args, *post_args)
    return out if post else out[0]


def _ab_in_stage(x, g_ref, win_ref, cdft_ref, vn_ref, ws_ref, bs_ref, z_ref, gated_ref):
    h = _rms(x, g_ref[...]).astype(BF16)
    z = _dot(h, win_ref[...])
    for g in range(N_GROUPS):
        fg = z[:, g * LANES:(g + 1) * LANES].astype(BF16)
        ab = _dot(fg, cdft_ref[...])
        z_ref[0, :, g * LANES:(g + 1) * LANES] = ab[:, :LANES].astype(BF16)
        z_ref[1, :, g * LANES:(g + 1) * LANES] = ab[:, LANES:].astype(BF16)
    uv = _gelu_tanh(z[:, MIX_W:])
    for g in range(N_GROUPS):
        u = uv[:, g * LANES:(g + 1) * LANES]
        v = uv[:, MIX_W + g * LANES:MIX_W + (g + 1) * LANES]
        vn = _rms(v, vn_ref[:, g * LANES:(g + 1) * LANES]).astype(BF16)
        for c in range(x.shape[0] // LANES):
            rows = slice(c * LANES, (c + 1) * LANES)
            s = _dot(ws_ref[g], vn[rows]) + bs_ref[g]
            gated_ref[rows, g * LANES:(g + 1) * LANES] = (u[rows] * s).astype(BF16)


def _ab_in_post(S, gain, w_in, cdft, v_norm, w_s, b_full, *, tm=FFN_TM):
    args = [gain.reshape(1, D_MODEL), w_in, cdft, v_norm.reshape(1, MIX_W), w_s, b_full]
    in_specs = [_const_spec(a.shape) for a in args]
    out_specs = [pl.BlockSpec((2, tm, MIX_W), lambda i: (0, i, 0)),
                 pl.BlockSpec((tm, MIX_W), lambda i: (i, 0))]
    out_shapes = [jax.ShapeDtypeStruct((2, S, MIX_W), BF16),
                  jax.ShapeDtypeStruct((S, MIX_W), BF16)]
    return _ab_in_stage, args, in_specs, out_specs, out_shapes


def _dft1_kernel(w_ref, z_ref, t_ref):
    t_ref[...] = _dot(w_ref[...], z_ref[...]).astype(BF16)


def _dft1(w1, zv, *, bn=4096):
    rows, cols = zv.shape
    return pl.pallas_call(
        _dft1_kernel,
        grid=(cols // bn,),
        in_specs=[_const_spec(w1.shape), pl.BlockSpec((rows, bn), lambda i: (0, i))],
        out_specs=pl.BlockSpec((rows, bn), lambda i: (0, i)),
        out_shape=jax.ShapeDtypeStruct((rows, cols), BF16),
        compiler_params=_params(1),
        name="dft_stage1",
    )(w1, zv)


def _dft2_kernel(g_ref, t_ref, o_ref, *, kb):
    for i in range(kb):
        tcat = jnp.concatenate([t_ref[0, i], t_ref[1, i]], axis=0)
        o_ref[:, i * MIX_W:(i + 1) * MIX_W] = _dot(g_ref[i], tcat).astype(BF16)


def _dft2(g2, t4, *, kb=8):
    _, n1, n2, w = t4.shape
    return pl.pallas_call(
        functools.partial(_dft2_kernel, kb=kb),
        grid=(n1 // kb,),
        in_specs=[
            pl.BlockSpec((kb, n2, 2 * n2), lambda i: (i, 0, 0)),
            pl.BlockSpec((2, kb, n2, w), lambda i: (0, i, 0, 0)),
        ],
        out_specs=pl.BlockSpec((n2, kb * w), lambda i: (0, i)),
        out_shape=jax.ShapeDtypeStruct((n2, n1 * w), BF16),
        compiler_params=_params(1),
        name="dft_stage2",
    )(g2, t4)


def _dft_tables(S):
    n1, n2 = S // LANES, LANES
    c = np.arange(LANES)
    ang = 2.0 * np.pi * np.outer(c, c) / LANES
    cdft = np.concatenate([np.cos(ang), np.sin(ang)], axis=1) / np.sqrt(LANES)
    a1 = 2.0 * np.pi * np.outer(np.arange(n1), np.arange(n1)) / n1
    c1, s1 = np.cos(a1), np.sin(a1)
    w1 = np.block([[c1, -s1], [-s1, -c1]]) / np.sqrt(n1)
    k = np.arange(n1)[:, None] + n1 * np.arange(n2)[None, :]
    a2 = 2.0 * np.pi * k[:, :, None] * np.arange(n2)[None, None, :] / S
    g2 = np.concatenate([np.cos(a2), np.sin(a2)], axis=-1) / np.sqrt(n2)
    to_bf16 = lambda a: jnp.asarray(a, dtype=F32).astype(BF16)
    return to_bf16(cdft), to_bf16(w1), to_bf16(g2)


def _qkv_stage(x, g_ref, w_ref, qn_ref, kn_ref, cos_ref, sin_ref, q_ref, kt_ref, v_ref):
    h = _rms(x, g_ref[...]).astype(BF16)
    qkv = _dot(h, w_ref[...])
    cosf = cos_ref[...]
    sinf = sin_ref[...]

    def head(col, gain):
        y = _rms(qkv[:, col:col + HEAD_DIM], gain)
        return y * cosf + pltpu.roll(y, HEAD_DIM // 2, 1) * sinf

    for i in range(N_HEADS):
        q_ref[:, i * HEAD_DIM:(i + 1) * HEAD_DIM] = (
            head(i * HEAD_DIM, qn_ref[...]) * Q_SCALE).astype(BF16)
    k0 = N_HEADS * HEAD_DIM
    v0 = k0 + N_KV_HEADS * HEAD_DIM
    for i in range(N_KV_HEADS):
        kt_ref[i * HEAD_DIM:(i + 1) * HEAD_DIM, :] = head(k0 + i * HEAD_DIM, kn_ref[...]).T.astype(BF16)
        vcol = 2 * i * HEAD_DIM
        v_ref[:, vcol:vcol + HEAD_DIM] = qkv[:, v0 + i * HEAD_DIM:v0 + (i + 1) * HEAD_DIM].astype(BF16)
        v_ref[:, vcol + HEAD_DIM:vcol + 2 * HEAD_DIM] = jnp.ones((x.shape[0], HEAD_DIM), BF16)


def _qkv_kernel(x_ref, *refs):
    _qkv_stage(x_ref[...], *refs)


def _qkv(x, gain, w, qn, kn, cosf, sinf, *, tm=256):
    S = x.shape[0]
    row = lambda wd: pl.BlockSpec((tm, wd), lambda i: (i, 0))
    args = [gain.reshape(1, D_MODEL), w, qn.reshape(1, HEAD_DIM), kn.reshape(1, HEAD_DIM)]
    return pl.pallas_call(
        _qkv_kernel,
        grid=(S // tm,),
        in_specs=[row(D_MODEL)] + [_const_spec(a.shape) for a in args]
        + [row(HEAD_DIM), row(HEAD_DIM)],
        out_specs=[row(N_HEADS * HEAD_DIM),
                   pl.BlockSpec((N_KV_HEADS * HEAD_DIM, tm), lambda i: (0, i)),
                   row(2 * N_KV_HEADS * HEAD_DIM)],
        out_shape=[jax.ShapeDtypeStruct((S, N_HEADS * HEAD_DIM), BF16),
                   jax.ShapeDtypeStruct((N_KV_HEADS * HEAD_DIM, S), BF16),
                   jax.ShapeDtypeStruct((S, 2 * N_KV_HEADS * HEAD_DIM), BF16)],
        compiler_params=_params(1),
        name="qkv_rope",
    )(x, *args, cosf, sinf)


def _flash_kernel(q_ref, kt_ref, v_ref, o_ref, q_sc, s_sc, mx_sc, p_sc, al_sc, m_sc, acc_sc,
                  *, tq, tk, nb):
    nk = kt_ref.shape[1] // tk
    n_tiles = q_ref.shape[0] // tq
    mq = KV_GROUP * tq
    n_chunks = n_tiles * nk
    for t in range(n_tiles):
        for g in range(KV_GROUP):
            q_sc[t * mq + g * tq:t * mq + (g + 1) * tq, :] = (
                q_ref[t * tq:(t + 1) * tq, g * HEAD_DIM:(g + 1) * HEAD_DIM])

    def scores(jj):
        row = pl.multiple_of((jj // nk) * mq, mq)
        off = pl.multiple_of((jj % nk) * tk, tk)
        s = _dot(q_sc[pl.ds(row, mq), :], kt_ref[:, pl.ds(off, tk)])
        mx = s[:, :LANES]
        for t in range(1, tk // LANES):
            mx = jnp.maximum(mx, s[:, t * LANES:(t + 1) * LANES])
        return s, mx

    def softmax(s, mx, m_old):
        m_new = jnp.maximum(m_old, jnp.max(mx, axis=-1, keepdims=True))
        alpha = jnp.exp2(m_old - m_new)
        p = jnp.exp2((s - jnp.tile(m_new, (1, tk // LANES))).astype(BF16))
        return p, alpha, m_new

    def pv(acc, p, alpha, jj):
        off = pl.multiple_of((jj % nk) * tk, tk)
        return acc * jnp.tile(alpha, (1, 2)) + _dot(p, v_ref[pl.ds(off, tk), :])

    def write_out(acc, tile):
        o = acc[:, :HEAD_DIM] / acc[:, HEAD_DIM:]
        row = pl.multiple_of(tile * tq, tq)
        for g in range(KV_GROUP):
            o_ref[pl.ds(row, tq), g * HEAD_DIM:(g + 1) * HEAD_DIM] = (
                o[g * tq:(g + 1) * tq].astype(BF16))

    s0, mx0 = scores(0)
    s_sc[...] = s0
    mx_sc[...] = mx0
    m_sc[...] = jnp.full(m_sc.shape, -1e30, F32)
    acc_sc[...] = jnp.concatenate([jnp.zeros((mq, HEAD_DIM), F32), jnp.ones((mq, HEAD_DIM), F32)],
                                  axis=1)
    p_sc[...] = jnp.zeros(p_sc.shape, BF16)
    al_sc[...] = jnp.ones(al_sc.shape, F32)

    def body(i, carry):
        j0 = nb * i
        s, mx, p, alpha = s_sc[...], mx_sc[...], p_sc[...], al_sc[...]
        acc = acc_sc[...]
        m = jnp.where(j0 % nk == 0, -1e30, m_sc[...])
        for c in range(nb):
            jj = j0 + c
            acc = pv(acc, p, alpha, jnp.maximum(jj - 1, 0) if c == 0 else jj - 1)
            if c == 0:
                write_out(acc, jnp.maximum(jj - 1, 0) // nk)
            s_next, mx_next = scores(jnp.minimum(jj + 1, n_chunks - 1) if c == nb - 1 else jj + 1)
            p, alpha, m = softmax(s, mx, m)
            s, mx = s_next, mx_next
        s_sc[...], mx_sc[...], p_sc[...], al_sc[...] = s, mx, p, alpha
        m_sc[...], acc_sc[...] = m, acc
        return carry

    lax.fori_loop(0, n_chunks // nb, body, 0)
    write_out(pv(acc_sc[...], p_sc[...], al_sc[...], n_chunks - 1), n_tiles - 1)


def _flash(q, k, v, *, tq=FLASH_TQ, tk=FLASH_TK, nb=FLASH_NB, n_tiles=FLASH_TILES):
    S = q.shape[0]
    gw = KV_GROUP * HEAD_DIM
    mq = KV_GROUP * tq
    nb = min(nb, S // tk)
    assert (S // tk) % nb == 0 and S % (n_tiles * tq) == 0
    return pl.pallas_call(
        functools.partial(_flash_kernel, tq=tq, tk=tk, nb=nb),
        grid=(N_KV_HEADS, S // (n_tiles * tq)),
        in_specs=[
            pl.BlockSpec((n_tiles * tq, gw), lambda h, i: (i, h)),
            pl.BlockSpec((HEAD_DIM, S), lambda h, i: (h, 0)),
            pl.BlockSpec((S, 2 * HEAD_DIM), lambda h, i: (0, h)),
        ],
        out_specs=pl.BlockSpec((n_tiles * tq, gw), lambda h, i: (i, h)),
        out_shape=jax.ShapeDtypeStruct((S, N_HEADS * HEAD_DIM), BF16),
        scratch_shapes=[
            pltpu.VMEM((n_tiles * mq, HEAD_DIM), BF16),
            pltpu.VMEM((mq, tk), F32),
            pltpu.VMEM((mq, LANES), F32),
            pltpu.VMEM((mq, tk), BF16),
            pltpu.VMEM((mq, LANES), F32),
            pltpu.VMEM((mq, LANES), F32),
            pltpu.VMEM((mq, 2 * HEAD_DIM), F32),
        ],
        compiler_params=_params(2),
        name="gqa_flash",
    )(q, k, v)


def _rope_tables(S):
    rows = S // GRID_W
    inv_freq = ROPE_THETA ** (-jnp.arange(ROPE_FREQS, dtype=F32) / ROPE_FREQS)

    def axis_tables(n):
        ang = jnp.arange(n).astype(F32)[:, None] * inv_freq[None, :]
        return jnp.cos(ang), jnp.sin(ang)

    def per_token(row_t, col_t):
        return jnp.concatenate([jnp.repeat(row_t, GRID_W, axis=0), jnp.tile(col_t, (rows, 1))],
                               axis=-1)

    (cos_r, sin_r), (cos_c, sin_c) = axis_tables(rows), axis_tables(GRID_W)
    cos, sin = per_token(cos_r, cos_c), per_token(sin_r, sin_c)
    return jnp.concatenate([cos, cos], axis=-1), jnp.concatenate([-sin, sin], axis=-1)


def kernel(x, ffn1_norm, ffn1_w_gate, ffn1_w_up, ffn1_w_down, mix_norm, ab_w_in, ab_v_norm, ab_w_s, ab_b_s, ab_w_out, attn_w_qkv, attn_q_norm, attn_k_norm, attn_w_o, ffn2_norm, ffn2_w_gate, ffn2_w_up, ffn2_w_down, final_norm):
    B, S, _ = x.shape
    assert B == 1 and S % 2048 == 0
    bf = lambda w: w.astype(BF16)
    xs = x.reshape(S, D_MODEL)
    ffn1_w = (bf(ffn1_w_gate), bf(ffn1_w_up), bf(ffn1_w_down))
    ffn2_w = (bf(ffn2_w_gate), bf(ffn2_w_up), bf(ffn2_w_down))

    cdft, w1, g2 = _dft_tables(S)
    n1 = S // LANES
    b_full = jnp.broadcast_to(ab_b_s[0][:, :, None], (N_GROUPS, LANES, LANES))
    ab_in = _ab_in_post(S, mix_norm[0], bf(ab_w_in[0]), cdft, ab_v_norm[0], bf(ab_w_s[0]), b_full)
    xs, z, gated = _ffn(xs, [], ffn1_norm[0], *ffn1_w, 0, post=ab_in, name="ffn1_abin_l0")
    t = _dft1(w1, z.reshape(2 * n1, LANES * MIX_W))
    f = _dft2(g2, t.reshape(2, n1, LANES, MIX_W)).reshape(S, MIX_W)
    w_out = bf(ab_w_out[0])
    xs = _ffn(xs, [(f, w_out, 0), (gated, w_out, 1)], ffn2_norm[0], *ffn2_w, 0,
              name="mixout_ffn2_l0")

    half = np.concatenate([np.arange(0, HEAD_DIM, 2), np.arange(1, HEAD_DIM, 2)])
    n_rot = N_HEADS + N_KV_HEADS
    cols = np.concatenate([(h * HEAD_DIM + half) for h in range(n_rot)]
                          + [np.arange(n_rot * HEAD_DIM, (n_rot + N_KV_HEADS) * HEAD_DIM)])
    w_qkv = bf(attn_w_qkv[0][:, cols])
    cosf, sinf = _rope_tables(S)
    xs = _ffn(xs, [], ffn1_norm[1], *ffn1_w, 1, name="ffn1_l1")
    q, k, v = _qkv(xs, mix_norm[1], w_qkv, attn_q_norm[0][half], attn_k_norm[0][half],
                   cosf, sinf)
    o = _flash(q, k, v)
    xs = _ffn(xs, [(o, bf(attn_w_o[0]), 0)], ffn2_norm[1], *ffn2_w, 1, final_gain=final_norm,
              name="attnout_ffn2_final")
    return xs.reshape(B, S, D_MODEL)
```

```python
import functools
import math

import numpy as np
import jax
import jax.numpy as jnp
from jax import lax
from jax.experimental import pallas as pl
from jax.experimental.pallas import tpu as pltpu

F32 = jnp.float32
BF16 = jnp.bfloat16

D_MODEL = 1024
D_FF = 2816
EPS = 1e-6
LANES = 128
N_GROUPS = 4
MIX_W = N_GROUPS * LANES
HEAD_DIM = 128
N_HEADS = 8
N_KV_HEADS = 2
KV_GROUP = N_HEADS // N_KV_HEADS
GRID_W = 64
ROPE_THETA = 10000.0
ROPE_FREQS = HEAD_DIM // 4
Q_SCALE = HEAD_DIM ** -0.5 * math.log2(math.e)
VMEM_LIMIT = 56 * 1024 * 1024
FFN_TM = 512
FLASH_TQ = 128
FLASH_TILES = 8
FLASH_TK = 512
FLASH_NB = 32


def _params(n_axes):
    return pltpu.CompilerParams(
        dimension_semantics=("arbitrary",) * n_axes, vmem_limit_bytes=VMEM_LIMIT)


def _const_spec(shape):
    nd = len(shape)
    return pl.BlockSpec(shape, lambda *_: (0,) * nd, pipeline_mode=pl.Buffered(1))


def _layer_spec(w, layer):
    return pl.BlockSpec((None,) + w.shape[1:], lambda *_: (layer, 0, 0),
                        pipeline_mode=pl.Buffered(1))


def _rms(x, gain):
    return x * lax.rsqrt(jnp.mean(x * x, axis=-1, keepdims=True) + EPS) * gain


def _dot(a, b):
    return jnp.dot(a, b, preferred_element_type=F32)


def _sigmoid(x):
    return 1.0 / (1.0 + jnp.exp(-x))


def _gelu_tanh(x):
    c = math.sqrt(2.0 / math.pi)
    return x * (0.5 * (1.0 + jnp.tanh(c * (x + 0.044715 * (x * x * x)))))


def _swiglu_residual(x, gain, wg_ref, wu_ref, wd_ref):
    h = _rms(x, gain).astype(BF16)
    gate = _dot(h, wg_ref[...])
    up = _dot(h, wu_ref[...])
    a = (gate * _sigmoid(gate) * up).astype(BF16)
    return x + 0.5 * _dot(a, wd_ref[...])


def _ffn_kernel(*refs, n_pre, final_norm, post_fn, n_post_in):
    n_in = 5 + 2 * n_pre + int(final_norm) + n_post_in
    x_ref = refs[0]
    pre = refs[1:1 + 2 * n_pre]
    g_ref, wg_ref, wu_ref, wd_ref = refs[1 + 2 * n_pre:5 + 2 * n_pre]
    o_ref = refs[n_in]
    x = x_ref[...]
    for i in range(n_pre):
        x = x + _dot(pre[2 * i][...], pre[2 * i + 1][...])
    y = _swiglu_residual(x, g_ref[...], wg_ref, wu_ref, wd_ref)
    if final_norm:
        y = _rms(y, refs[5 + 2 * n_pre][...])
    o_ref[...] = y
    if post_fn is not None:
        post_fn(y, *refs[n_in - n_post_in:n_in], *refs[n_in + 1:])


def _ffn(x, pre, gain, wg, wu, wd, layer, final_gain=None, post=None, *, tm=FFN_TM, name="ffn"):
    S = x.shape[0]
    row = lambda w: pl.BlockSpec((tm, w), lambda i: (i, 0))
    args, specs = [x], [row(D_MODEL)]
    for a, w, blk in pre:
        args += [a, w]
        specs += [row(a.shape[1]),
                  pl.BlockSpec((a.shape[1], D_MODEL), lambda i, blk=blk: (blk, 0),
                               pipeline_mode=pl.Buffered(1))]
    args += [gain.reshape(1, D_MODEL), wg, wu, wd]
    specs += [_const_spec((1, D_MODEL)), _layer_spec(wg, layer), _layer_spec(wu, layer),
              _layer_spec(wd, layer)]
    if final_gain is not None:
        args.append(final_gain.reshape(1, D_MODEL))
        specs.append(_const_spec((1, D_MODEL)))
    post_fn, post_args, post_in, post_out, post_shapes = post or (None, [], [], [], [])
    out = pl.pallas_call(
        functools.partial(_ffn_kernel, n_pre=len(pre), final_norm=final_gain is not None,
                          post_fn=post_fn, n_post_in=len(post_args)),
        grid=(S // tm,),
        in_specs=specs + post_in,
        out_specs=[row(D_MODEL)] + post_out,
        out_shape=[jax.ShapeDtypeStruct((S, D_MODEL), F32)] + post_shapes,
        compiler_params=_params(1),
        name=name,
    )(*args, *post_args)
    return out if post else out[0]


def _ab_in_stage(x, g_ref, win_ref, cdft_ref, vn_ref, ws_ref, bs_ref, z_ref, gated_ref):
    h = _rms(x, g_ref[...]).astype(BF16)
    z = _dot(h, win_ref[...])
    for g in range(N_GROUPS):
        fg = z[:, g * LANES:(g + 1) * LANES].astype(BF16)
        ab = _dot(fg, cdft_ref[...])
        z_ref[0, :, g * LANES:(g + 1) * LANES] = ab[:, :LANES].astype(BF16)
        z_ref[1, :, g * LANES:(g + 1) * LANES] = ab[:, LANES:].astype(BF16)
    uv = _gelu_tanh(z[:, MIX_W:])
    for g in range(N_GROUPS):
        u = uv[:, g * LANES:(g + 1) * LANES]
        v = uv[:, MIX_W + g * LANES:MIX_W + (g + 1) * LANES]
        vn = _rms(v, vn_ref[:, g * LANES:(g + 1) * LANES]).astype(BF16)
        for c in range(x.shape[0] // LANES):
            rows = slice(c * LANES, (c + 1) * LANES)
            s = _dot(ws_ref[g], vn[rows]) + bs_ref[g]
            gated_ref[rows, g * LANES:(g + 1) * LANES] = (u[rows] * s).astype(BF16)


def _ab_in_post(S, gain, w_in, cdft, v_norm, w_s, b_full, *, tm=FFN_TM):
    args = [gain.reshape(1, D_MODEL), w_in, cdft, v_norm.reshape(1, MIX_W), w_s, b_full]
    in_specs = [_const_spec(a.shape) for a in args]
    out_specs = [pl.BlockSpec((2, tm, MIX_W), lambda i: (0, i, 0)),
                 pl.BlockSpec((tm, MIX_W), lambda i: (i, 0))]
    out_shapes = [jax.ShapeDtypeStruct((2, S, MIX_W), BF16),
                  jax.ShapeDtypeStruct((S, MIX_W), BF16)]
    return _ab_in_stage, args, in_specs, out_specs, out_shapes


def _dft1_kernel(w_ref, z_ref, t_ref):
    t_ref[...] = _dot(w_ref[...], z_ref[...]).astype(BF16)


def _dft1(w1, zv, *, bn=4096):
    rows, cols = zv.shape
    return pl.pallas_call(
        _dft1_kernel,
        grid=(cols // bn,),
        in_specs=[_const_spec(w1.shape), pl.BlockSpec((rows, bn), lambda i: (0, i))],
        out_specs=pl.BlockSpec((rows, bn), lambda i: (0, i)),
        out_shape=jax.ShapeDtypeStruct((rows, cols), BF16),
        compiler_params=_params(1),
        name="dft_stage1",
    )(w1, zv)


def _dft2_kernel(g_ref, t_ref, o_ref, *, kb):
    for i in range(kb):
        tcat = jnp.concatenate([t_ref[0, i], t_ref[1, i]], axis=0)
        o_ref[:, i * MIX_W:(i + 1) * MIX_W] = _dot(g_ref[i], tcat).astype(BF16)


def _dft2(g2, t4, *, kb=8):
    _, n1, n2, w = t4.shape
    return pl.pallas_call(
        functools.partial(_dft2_kernel, kb=kb),
        grid=(n1 // kb,),
        in_specs=[
            pl.BlockSpec((kb, n2, 2 * n2), lambda i: (i, 0, 0)),
            pl.BlockSpec((2, kb, n2, w), lambda i: (0, i, 0, 0)),
        ],
        out_specs=pl.BlockSpec((n2, kb * w), lambda i: (0, i)),
        out_shape=jax.ShapeDtypeStruct((n2, n1 * w), BF16),
        compiler_params=_params(1),
        name="dft_stage2",
    )(g2, t4)


def _dft_tables(S):
    n1, n2 = S // LANES, LANES
    c = np.arange(LANES)
    ang = 2.0 * np.pi * np.outer(c, c) / LANES
    cdft = np.concatenate([np.cos(ang), np.sin(ang)], axis=1) / np.sqrt(LANES)
    a1 = 2.0 * np.pi * np.outer(np.arange(n1), np.arange(n1)) / n1
    c1, s1 = np.cos(a1), np.sin(a1)
    w1 = np.block([[c1, -s1], [-s1, -c1]]) / np.sqrt(n1)
    k = np.arange(n1)[:, None] + n1 * np.arange(n2)[None, :]
    a2 = 2.0 * np.pi * k[:, :, None] * np.arange(n2)[None, None, :] / S
    g2 = np.concatenate([np.cos(a2), np.sin(a2)], axis=-1) / np.sqrt(n2)
    to_bf16 = lambda a: jnp.asarray(a, dtype=F32).astype(BF16)
    return to_bf16(cdft), to_bf16(w1), to_bf16(g2)


def _qkv_stage(x, g_ref, w_ref, qn_ref, kn_ref, rowtab_ref, coltab_ref, q_ref, kt_ref, v_ref):
    h = _rms(x, g_ref[...]).astype(BF16)
    qkv = _dot(h, w_ref[...])
    rows_per_tile = x.shape[0] // GRID_W
    row0 = pl.program_id(0) * rows_per_tile
    tab = jnp.concatenate(
        [jnp.broadcast_to(rowtab_ref[pl.ds(row0 + r, 1), :], coltab_ref.shape) + coltab_ref[...]
         for r in range(rows_per_tile)], axis=0)
    cosf = tab[:, :HEAD_DIM]
    sinf = tab[:, HEAD_DIM:]

    def head(col, gain):
        y = _rms(qkv[:, col:col + HEAD_DIM], gain)
        return y * cosf + pltpu.roll(y, HEAD_DIM // 2, 1) * sinf

    for i in range(N_HEADS):
        q_ref[:, i * HEAD_DIM:(i + 1) * HEAD_DIM] = (
            head(i * HEAD_DIM, qn_ref[...]) * Q_SCALE).astype(BF16)
    k0 = N_HEADS * HEAD_DIM
    v0 = k0 + N_KV_HEADS * HEAD_DIM
    for i in range(N_KV_HEADS):
        kt_ref[i * HEAD_DIM:(i + 1) * HEAD_DIM, :] = head(k0 + i * HEAD_DIM, kn_ref[...]).T.astype(BF16)
        vcol = 2 * i * HEAD_DIM
        v_ref[:, vcol:vcol + HEAD_DIM] = qkv[:, v0 + i * HEAD_DIM:v0 + (i + 1) * HEAD_DIM].astype(BF16)
        v_ref[:, vcol + HEAD_DIM:vcol + 2 * HEAD_DIM] = jnp.ones((x.shape[0], HEAD_DIM), BF16)


def _qkv_kernel(x_ref, *refs):
    _qkv_stage(x_ref[...], *refs)


def _qkv(x, gain, w, qn, kn, rowtab, coltab, *, tm=256):
    S = x.shape[0]
    assert tm % GRID_W == 0
    row = lambda wd: pl.BlockSpec((tm, wd), lambda i: (i, 0))
    args = [gain.reshape(1, D_MODEL), w, qn.reshape(1, HEAD_DIM), kn.reshape(1, HEAD_DIM),
            rowtab, coltab]
    return pl.pallas_call(
        _qkv_kernel,
        grid=(S // tm,),
        in_specs=[row(D_MODEL)] + [_const_spec(a.shape) for a in args],
        out_specs=[row(N_HEADS * HEAD_DIM),
                   pl.BlockSpec((N_KV_HEADS * HEAD_DIM, tm), lambda i: (0, i)),
                   row(2 * N_KV_HEADS * HEAD_DIM)],
        out_shape=[jax.ShapeDtypeStruct((S, N_HEADS * HEAD_DIM), BF16),
                   jax.ShapeDtypeStruct((N_KV_HEADS * HEAD_DIM, S), BF16),
                   jax.ShapeDtypeStruct((S, 2 * N_KV_HEADS * HEAD_DIM), BF16)],
        compiler_params=_params(1),
        name="qkv_rope",
    )(x, *args)


def _flash_kernel(q_ref, kt_ref, v_ref, o_ref, q_sc, s_sc, mx_sc, p_sc, al_sc, m_sc, acc_sc,
                  *, tq, tk, nb):
    nk = kt_ref.shape[1] // tk
    n_tiles = q_ref.shape[0] // tq
    mq = KV_GROUP * tq
    n_chunks = n_tiles * nk
    for t in range(n_tiles):
        for g in range(KV_GROUP):
            q_sc[t * mq + g * tq:t * mq + (g + 1) * tq, :] = (
                q_ref[t * tq:(t + 1) * tq, g * HEAD_DIM:(g + 1) * HEAD_DIM])

    def scores(jj):
        row = pl.multiple_of((jj // nk) * mq, mq)
        off = pl.multiple_of((jj % nk) * tk, tk)
        s = _dot(q_sc[pl.ds(row, mq), :], kt_ref[:, pl.ds(off, tk)])
        mx = s[:, :LANES]
        for t in range(1, tk // LANES):
            mx = jnp.maximum(mx, s[:, t * LANES:(t + 1) * LANES])
        return s, mx

    def softmax(s, mx, m_old):
        m_new = jnp.maximum(m_old, jnp.max(mx, axis=-1, keepdims=True))
        alpha = jnp.exp2(m_old - m_new)
        p = jnp.exp2((s - jnp.tile(m_new, (1, tk // LANES))).astype(BF16))
        return p, alpha, m_new

    def pv(acc, p, alpha, jj):
        off = pl.multiple_of((jj % nk) * tk, tk)
        return acc * jnp.tile(alpha, (1, 2)) + _dot(p, v_ref[pl.ds(off, tk), :])

    def write_out(acc, tile):
        o = acc[:, :HEAD_DIM] / acc[:, HEAD_DIM:]
        row = pl.multiple_of(tile * tq, tq)
        for g in range(KV_GROUP):
            o_ref[pl.ds(row, tq), g * HEAD_DIM:(g + 1) * HEAD_DIM] = (
                o[g * tq:(g + 1) * tq].astype(BF16))

    s0, mx0 = scores(0)
    s_sc[...] = s0
    mx_sc[...] = mx0
    m_sc[...] = jnp.full(m_sc.shape, -1e30, F32)
    acc_sc[...] = jnp.concatenate([jnp.zeros((mq, HEAD_DIM), F32), jnp.ones((mq, HEAD_DIM), F32)],
                                  axis=1)
    p_sc[...] = jnp.zeros(p_sc.shape, BF16)
    al_sc[...] = jnp.ones(al_sc.shape, F32)

    def body(i, carry):
        j0 = nb * i
        s, mx, p, alpha = s_sc[...], mx_sc[...], p_sc[...], al_sc[...]
        acc = acc_sc[...]
        m = jnp.where(j0 % nk == 0, -1e30, m_sc[...])
        for c in range(nb):
            jj = j0 + c
            acc = pv(acc, p, alpha, jnp.maximum(jj - 1, 0) if c == 0 else jj - 1)
            if c == 0:
                write_out(acc, jnp.maximum(jj - 1, 0) // nk)
            s_next, mx_next = scores(jnp.minimum(jj + 1, n_chunks - 1) if c == nb - 1 else jj + 1)
            p, alpha, m = softmax(s, mx, m)
            s, mx = s_next, mx_next
        s_sc[...], mx_sc[...], p_sc[...], al_sc[...] = s, mx, p, alpha
        m_sc[...], acc_sc[...] = m, acc
        return carry

    lax.fori_loop(0, n_chunks // nb, body, 0)
    write_out(pv(acc_sc[...], p_sc[...], al_sc[...], n_chunks - 1), n_tiles - 1)


def _flash(q, k, v, *, tq=FLASH_TQ, tk=FLASH_TK, nb=FLASH_NB, n_tiles=FLASH_TILES):
    S = q.shape[0]
    gw = KV_GROUP * HEAD_DIM
    mq = KV_GROUP * tq
    nb = min(nb, S // tk)
    assert (S // tk) % nb == 0 and S % (n_tiles * tq) == 0
    return pl.pallas_call(
        functools.partial(_flash_kernel, tq=tq, tk=tk, nb=nb),
        grid=(N_KV_HEADS, S // (n_tiles * tq)),
        in_specs=[
            pl.BlockSpec((n_tiles * tq, gw), lambda h, i: (i, h)),
            pl.BlockSpec((HEAD_DIM, S), lambda h, i: (h, 0)),
            pl.BlockSpec((S, 2 * HEAD_DIM), lambda h, i: (0, h)),
        ],
        out_specs=pl.BlockSpec((n_tiles * tq, gw), lambda h, i: (i, h)),
        out_shape=jax.ShapeDtypeStruct((S, N_HEADS * HEAD_DIM), BF16),
        scratch_shapes=[
            pltpu.VMEM((n_tiles * mq, HEAD_DIM), BF16),
            pltpu.VMEM((mq, tk), F32),
            pltpu.VMEM((mq, LANES), F32),
            pltpu.VMEM((mq, tk), BF16),
            pltpu.VMEM((mq, LANES), F32),
            pltpu.VMEM((mq, LANES), F32),
            pltpu.VMEM((mq, 2 * HEAD_DIM), F32),
        ],
        compiler_params=_params(2),
        name="gqa_flash",
    )(q, k, v)


def _rope_tables(S):
    inv_freq = ROPE_THETA ** (-jnp.arange(ROPE_FREQS, dtype=F32) / ROPE_FREQS)

    def axis_table(n, row_axis):
        ang = jnp.arange(n).astype(F32)[:, None] * inv_freq[None, :]
        cos, sin, zero = jnp.cos(ang), jnp.sin(ang), jnp.zeros_like(ang)
        pair = (lambda a: [a, zero]) if row_axis else (lambda a: [zero, a])
        return jnp.concatenate(pair(cos) + pair(cos) + pair(-sin) + pair(sin), axis=-1)

    return axis_table(S // GRID_W, True), axis_table(GRID_W, False)


def kernel(x, ffn1_norm, ffn1_w_gate, ffn1_w_up, ffn1_w_down, mix_norm, ab_w_in, ab_v_norm, ab_w_s, ab_b_s, ab_w_out, attn_w_qkv, attn_q_norm, attn_k_norm, attn_w_o, ffn2_norm, ffn2_w_gate, ffn2_w_up, ffn2_w_down, final_norm):
    B, S, _ = x.shape
    assert B == 1 and S % 2048 == 0
    bf = lambda w: w.astype(BF16)
    xs = x.reshape(S, D_MODEL)
    ffn1_w = (bf(ffn1_w_gate), bf(ffn1_w_up), bf(ffn1_w_down))
    ffn2_w = (bf(ffn2_w_gate), bf(ffn2_w_up), bf(ffn2_w_down))

    cdft, w1, g2 = _dft_tables(S)
    n1 = S // LANES
    b_full = jnp.broadcast_to(ab_b_s[0][:, :, None], (N_GROUPS, LANES, LANES))
    ab_in = _ab_in_post(S, mix_norm[0], bf(ab_w_in[0]), cdft, ab_v_norm[0], bf(ab_w_s[0]), b_full)
    xs, z, gated = _ffn(xs, [], ffn1_norm[0], *ffn1_w, 0, post=ab_in, name="ffn1_abin_l0")
    t = _dft1(w1, z.reshape(2 * n1, LANES * MIX_W))
    f = _dft2(g2, t.reshape(2, n1, LANES, MIX_W)).reshape(S, MIX_W)
    w_out = bf(ab_w_out[0])
    xs = _ffn(xs, [(f, w_out, 0), (gated, w_out, 1)], ffn2_norm[0], *ffn2_w, 0,
              name="mixout_ffn2_l0")

    half = np.concatenate([np.arange(0, HEAD_DIM, 2), np.arange(1, HEAD_DIM, 2)])
    n_rot = N_HEADS + N_KV_HEADS
    cols = np.concatenate([(h * HEAD_DIM + half) for h in range(n_rot)]
                          + [np.arange(n_rot * HEAD_DIM, (n_rot + N_KV_HEADS) * HEAD_DIM)])
    w_qkv = bf(attn_w_qkv[0][:, cols])
    xs = _ffn(xs, [], ffn1_norm[1], *ffn1_w, 1, name="ffn1_l1")
    q, k, v = _qkv(xs, mix_norm[1], w_qkv, attn_q_norm[0][half], attn_k_norm[0][half],
                   *_rope_tables(S))
    o = _flash(q, k, v)
    xs = _ffn(xs, [(o, bf(attn_w_o[0]), 0)], ffn2_norm[1], *ffn2_w, 1, final_gain=final_norm,
              name="attnout_ffn2_final")
    return xs.reshape(B, S, D_MODEL)
```

```python
import functools
import math

import numpy as np
import jax
import jax.numpy as jnp
from jax import lax
from jax.experimental import pallas as pl
from jax.experimental.pallas import tpu as pltpu

F32 = jnp.float32
BF16 = jnp.bfloat16

D_MODEL = 1024
D_FF = 2816
EPS = 1e-6
LANES = 128
N_GROUPS = 4
MIX_W = N_GROUPS * LANES
HEAD_DIM = 128
N_HEADS = 8
N_KV_HEADS = 2
KV_GROUP = N_HEADS // N_KV_HEADS
GRID_W = 64
ROPE_THETA = 10000.0
ROPE_FREQS = HEAD_DIM // 4
Q_SCALE = HEAD_DIM ** -0.5 * math.log2(math.e)
VMEM_LIMIT = 56 * 1024 * 1024
FFN_TM = 512
FLASH_TQ = 128
FLASH_TILES = 8
FLASH_TK = 1024
FLASH_NB = 16


def _params(n_axes):
    return pltpu.CompilerParams(
        dimension_semantics=("arbitrary",) * n_axes, vmem_limit_bytes=VMEM_LIMIT)


def _const_spec(shape):
    nd = len(shape)
    return pl.BlockSpec(shape, lambda *_: (0,) * nd, pipeline_mode=pl.Buffered(1))


def _layer_spec(w, layer):
    return pl.BlockSpec((None,) + w.shape[1:], lambda *_: (layer, 0, 0),
                        pipeline_mode=pl.Buffered(1))


def _rms(x, gain):
    return x * lax.rsqrt(jnp.mean(x * x, axis=-1, keepdims=True) + EPS) * gain


def _dot(a, b):
    return jnp.dot(a, b, preferred_element_type=F32)


def _sigmoid(x):
    return 1.0 / (1.0 + jnp.exp(-x))


def _gelu_tanh(x):
    c = math.sqrt(2.0 / math.pi)
    return x * (0.5 * (1.0 + jnp.tanh(c * (x + 0.044715 * (x * x * x)))))


def _swiglu_residual(x, gain, wg_ref, wu_ref, wd_ref):
    h = _rms(x, gain).astype(BF16)
    gate = _dot(h, wg_ref[...])
    up = _dot(h, wu_ref[...])
    a = (gate * _sigmoid(gate) * up).astype(BF16)
    return x + 0.5 * _dot(a, wd_ref[...])


def _ffn_kernel(*refs, n_pre, final_norm, post_fn, n_post_in):
    n_in = 5 + 2 * n_pre + int(final_norm) + n_post_in
    x_ref = refs[0]
    pre = refs[1:1 + 2 * n_pre]
    g_ref, wg_ref, wu_ref, wd_ref = refs[1 + 2 * n_pre:5 + 2 * n_pre]
    o_ref = refs[n_in]
    x = x_ref[...]
    for i in range(n_pre):
        x = x + _dot(pre[2 * i][...], pre[2 * i + 1][...])
    y = _swiglu_residual(x, g_ref[...], wg_ref, wu_ref, wd_ref)
    if final_norm:
        y = _rms(y, refs[5 + 2 * n_pre][...])
    o_ref[...] = y
    if post_fn is not None:
        post_fn(y, *refs[n_in - n_post_in:n_in], *refs[n_in + 1:])


def _ffn(x, pre, gain, wg, wu, wd, layer, final_gain=None, post=None, *, tm=FFN_TM, name="ffn"):
    S = x.shape[0]
    row = lambda w: pl.BlockSpec((tm, w), lambda i: (i, 0))
    args, specs = [x], [row(D_MODEL)]
    for a, w, blk in pre:
        args += [a, w]
        specs += [row(a.shape[1]),
                  pl.BlockSpec((a.shape[1], D_MODEL), lambda i, blk=blk: (blk, 0),
                               pipeline_mode=pl.Buffered(1))]
    args += [gain.reshape(1, D_MODEL), wg, wu, wd]
    specs += [_const_spec((1, D_MODEL)), _layer_spec(wg, layer), _layer_spec(wu, layer),
              _layer_spec(wd, layer)]
    if final_gain is not None:
        args.append(final_gain.reshape(1, D_MODEL))
        specs.append(_const_spec((1, D_MODEL)))
    post_fn, post_args, post_in, post_out, post_shapes = post or (None, [], [], [], [])
    out = pl.pallas_call(
        functools.partial(_ffn_kernel, n_pre=len(pre), final_norm=final_gain is not None,
                          post_fn=post_fn, n_post_in=len(post_args)),
        grid=(S // tm,),
        in_specs=specs + post_in,
        out_specs=[row(D_MODEL)] + post_out,
        out_shape=[jax.ShapeDtypeStruct((S, D_MODEL), F32)] + post_shapes,
        compiler_params=_params(1),
        name=name,
    )(*args, *post_args)
    return out if post else out[0]


def _ab_in_stage(x, g_ref, win_ref, cdft_ref, vn_ref, ws_ref, bs_ref, z_ref, gated_ref):
    h = _rms(x, g_ref[...]).astype(BF16)
    z = _dot(h, win_ref[...])
    for g in range(N_GROUPS):
        fg = z[:, g * LANES:(g + 1) * LANES].astype(BF16)
        ab = _dot(fg, cdft_ref[...])
        z_ref[0, :, g * LANES:(g + 1) * LANES] = ab[:, :LANES].astype(BF16)
        z_ref[1, :, g * LANES:(g + 1) * LANES] = ab[:, LANES:].astype(BF16)
    uv = _gelu_tanh(z[:, MIX_W:])
    for g in range(N_GROUPS):
        u = uv[:, g * LANES:(g + 1) * LANES]
        v = uv[:, MIX_W + g * LANES:MIX_W + (g + 1) * LANES]
        vn = _rms(v, vn_ref[:, g * LANES:(g + 1) * LANES]).astype(BF16)
        for c in range(x.shape[0] // LANES):
            rows = slice(c * LANES, (c + 1) * LANES)
            s = _dot(ws_ref[g], vn[rows]) + bs_ref[g]
            gated_ref[rows, g * LANES:(g + 1) * LANES] = (u[rows] * s).astype(BF16)


def _ab_in_post(S, gain, w_in, cdft, v_norm, w_s, b_full, *, tm=FFN_TM):
    args = [gain.reshape(1, D_MODEL), w_in, cdft, v_norm.reshape(1, MIX_W), w_s, b_full]
    in_specs = [_const_spec(a.shape) for a in args]
    out_specs = [pl.BlockSpec((2, tm, MIX_W), lambda i: (0, i, 0)),
                 pl.BlockSpec((tm, MIX_W), lambda i: (i, 0))]
    out_shapes = [jax.ShapeDtypeStruct((2, S, MIX_W), BF16),
                  jax.ShapeDtypeStruct((S, MIX_W), BF16)]
    return _ab_in_stage, args, in_specs, out_specs, out_shapes


def _dft1_kernel(w_ref, z_ref, t_ref):
    t_ref[...] = _dot(w_ref[...], z_ref[...]).astype(BF16)


def _dft1(w1, zv, *, bn=16384):
    rows, cols = zv.shape
    return pl.pallas_call(
        _dft1_kernel,
        grid=(cols // bn,),
        in_specs=[_const_spec(w1.shape), pl.BlockSpec((rows, bn), lambda i: (0, i))],
        out_specs=pl.BlockSpec((rows, bn), lambda i: (0, i)),
        out_shape=jax.ShapeDtypeStruct((rows, cols), BF16),
        compiler_params=_params(1),
        name="dft_stage1",
    )(w1, zv)


def _dft2_kernel(g_ref, t_ref, o_ref, *, kb):
    for i in range(kb):
        tcat = jnp.concatenate([t_ref[0, i], t_ref[1, i]], axis=0)
        o_ref[:, i * MIX_W:(i + 1) * MIX_W] = _dot(g_ref[i], tcat).astype(BF16)


def _dft2(g2, t4, *, kb=32):
    _, n1, n2, w = t4.shape
    return pl.pallas_call(
        functools.partial(_dft2_kernel, kb=kb),
        grid=(n1 // kb,),
        in_specs=[
            pl.BlockSpec((kb, n2, 2 * n2), lambda i: (i, 0, 0)),
            pl.BlockSpec((2, kb, n2, w), lambda i: (0, i, 0, 0)),
        ],
        out_specs=pl.BlockSpec((n2, kb * w), lambda i: (0, i)),
        out_shape=jax.ShapeDtypeStruct((n2, n1 * w), BF16),
        compiler_params=_params(1),
        name="dft_stage2",
    )(g2, t4)


def _dft_tables(S):
    n1, n2 = S // LANES, LANES
    c = np.arange(LANES)
    ang = 2.0 * np.pi * np.outer(c, c) / LANES
    cdft = np.concatenate([np.cos(ang), np.sin(ang)], axis=1) / np.sqrt(LANES)
    a1 = 2.0 * np.pi * np.outer(np.arange(n1), np.arange(n1)) / n1
    c1, s1 = np.cos(a1), np.sin(a1)
    w1 = np.block([[c1, -s1], [-s1, -c1]]) / np.sqrt(n1)
    k = np.arange(n1)[:, None] + n1 * np.arange(n2)[None, :]
    a2 = 2.0 * np.pi * k[:, :, None] * np.arange(n2)[None, None, :] / S
    g2 = np.concatenate([np.cos(a2), np.sin(a2)], axis=-1) / np.sqrt(n2)
    to_bf16 = lambda a: jnp.asarray(a, dtype=F32).astype(BF16)
    return to_bf16(cdft), to_bf16(w1), to_bf16(g2)


def _qkv_stage(x, g_ref, w_ref, qn_ref, kn_ref, rowtab_ref, coltab_ref, q_ref, kt_ref, v_ref):
    h = _rms(x, g_ref[...]).astype(BF16)
    qkv = _dot(h, w_ref[...])
    rows_per_tile = x.shape[0] // GRID_W
    row0 = pl.program_id(0) * rows_per_tile
    tab = jnp.concatenate(
        [jnp.broadcast_to(rowtab_ref[pl.ds(row0 + r, 1), :], coltab_ref.shape) + coltab_ref[...]
         for r in range(rows_per_tile)], axis=0)
    cosf = tab[:, :HEAD_DIM]
    sinf = tab[:, HEAD_DIM:]

    def head(col, gain):
        y = _rms(qkv[:, col:col + HEAD_DIM], gain)
        return y * cosf + pltpu.roll(y, HEAD_DIM // 2, 1) * sinf

    for i in range(N_HEADS):
        q_ref[:, i * HEAD_DIM:(i + 1) * HEAD_DIM] = (
            head(i * HEAD_DIM, qn_ref[...]) * Q_SCALE).astype(BF16)
    k0 = N_HEADS * HEAD_DIM
    v0 = k0 + N_KV_HEADS * HEAD_DIM
    for i in range(N_KV_HEADS):
        kt_ref[i * HEAD_DIM:(i + 1) * HEAD_DIM, :] = head(k0 + i * HEAD_DIM, kn_ref[...]).T.astype(BF16)
        vcol = 2 * i * HEAD_DIM
        v_ref[:, vcol:vcol + HEAD_DIM] = qkv[:, v0 + i * HEAD_DIM:v0 + (i + 1) * HEAD_DIM].astype(BF16)
        v_ref[:, vcol + HEAD_DIM:vcol + 2 * HEAD_DIM] = jnp.ones((x.shape[0], HEAD_DIM), BF16)


def _qkv_kernel(x_ref, *refs):
    _qkv_stage(x_ref[...], *refs)


def _qkv(x, gain, w, qn, kn, rowtab, coltab, *, tm=256):
    S = x.shape[0]
    assert tm % GRID_W == 0
    row = lambda wd: pl.BlockSpec((tm, wd), lambda i: (i, 0))
    args = [gain.reshape(1, D_MODEL), w, qn.reshape(1, HEAD_DIM), kn.reshape(1, HEAD_DIM),
            rowtab, coltab]
    return pl.pallas_call(
        _qkv_kernel,
        grid=(S // tm,),
        in_specs=[row(D_MODEL)] + [_const_spec(a.shape) for a in args],
        out_specs=[row(N_HEADS * HEAD_DIM),
                   pl.BlockSpec((N_KV_HEADS * HEAD_DIM, tm), lambda i: (0, i)),
                   row(2 * N_KV_HEADS * HEAD_DIM)],
        out_shape=[jax.ShapeDtypeStruct((S, N_HEADS * HEAD_DIM), BF16),
                   jax.ShapeDtypeStruct((N_KV_HEADS * HEAD_DIM, S), BF16),
                   jax.ShapeDtypeStruct((S, 2 * N_KV_HEADS * HEAD_DIM), BF16)],
        compiler_params=_params(1),
        name="qkv_rope",
    )(x, *args)


def _flash_kernel(q_ref, kt_ref, v_ref, o_ref, q_sc, s_sc, mx_sc, p_sc, al_sc, m_sc, acc_sc,
                  *, tq, tk, nb):
    nk = kt_ref.shape[1] // tk
    n_tiles = q_ref.shape[0] // tq
    mq = KV_GROUP * tq
    n_chunks = n_tiles * nk
    for t in range(n_tiles):
        for g in range(KV_GROUP):
            q_sc[t * mq + g * tq:t * mq + (g + 1) * tq, :] = (
                q_ref[t * tq:(t + 1) * tq, g * HEAD_DIM:(g + 1) * HEAD_DIM])

    def scores(jj):
        row = pl.multiple_of((jj // nk) * mq, mq)
        off = pl.multiple_of((jj % nk) * tk, tk)
        s = _dot(q_sc[pl.ds(row, mq), :], kt_ref[:, pl.ds(off, tk)])
        mx = s[:, :LANES]
        for t in range(1, tk // LANES):
            mx = jnp.maximum(mx, s[:, t * LANES:(t + 1) * LANES])
        return s, mx

    def softmax(s, mx, m_old):
        m_new = jnp.maximum(m_old, jnp.max(mx, axis=-1, keepdims=True))
        alpha = jnp.exp2(m_old - m_new)
        p = jnp.exp2((s - jnp.tile(m_new, (1, tk // LANES))).astype(BF16))
        return p, alpha, m_new

    def pv(acc, p, alpha, jj):
        off = pl.multiple_of((jj % nk) * tk, tk)
        return acc * jnp.tile(alpha, (1, 2)) + _dot(p, v_ref[pl.ds(off, tk), :])

    def write_out(acc, tile):
        o = acc[:, :HEAD_DIM] / acc[:, HEAD_DIM:]
        row = pl.multiple_of(tile * tq, tq)
        for g in range(KV_GROUP):
            o_ref[pl.ds(row, tq), g * HEAD_DIM:(g + 1) * HEAD_DIM] = (
                o[g * tq:(g + 1) * tq].astype(BF16))

    s0, mx0 = scores(0)
    s_sc[...] = s0
    mx_sc[...] = mx0
    m_sc[...] = jnp.full(m_sc.shape, -1e30, F32)
    acc_sc[...] = jnp.concatenate([jnp.zeros((mq, HEAD_DIM), F32), jnp.ones((mq, HEAD_DIM), F32)],
                                  axis=1)
    p_sc[...] = jnp.zeros(p_sc.shape, BF16)
    al_sc[...] = jnp.ones(al_sc.shape, F32)

    def body(i, carry):
        j0 = nb * i
        s, mx, p, alpha = s_sc[...], mx_sc[...], p_sc[...], al_sc[...]
        acc = acc_sc[...]
        m = jnp.where(j0 % nk == 0, -1e30, m_sc[...])
        for c in range(nb):
            jj = j0 + c
            acc = pv(acc, p, alpha, jnp.maximum(jj - 1, 0) if c == 0 else jj - 1)
            if c == 0:
                write_out(acc, jnp.maximum(jj - 1, 0) // nk)
            s_next, mx_next = scores(jnp.minimum(jj + 1, n_chunks - 1) if c == nb - 1 else jj + 1)
            p, alpha, m = softmax(s, mx, m)
            s, mx = s_next, mx_next
        s_sc[...], mx_sc[...], p_sc[...], al_sc[...] = s, mx, p, alpha
        m_sc[...], acc_sc[...] = m, acc
        return carry

    lax.fori_loop(0, n_chunks // nb, body, 0)
    write_out(pv(acc_sc[...], p_sc[...], al_sc[...], n_chunks - 1), n_tiles - 1)


def _flash(q, k, v, *, tq=FLASH_TQ, tk=FLASH_TK, nb=FLASH_NB, n_tiles=FLASH_TILES):
    S = q.shape[0]
    gw = KV_GROUP * HEAD_DIM
    mq = KV_GROUP * tq
    nb = min(nb, S // tk)
    assert (S // tk) % nb == 0 and S % (n_tiles * tq) == 0
    return pl.pallas_call(
        functools.partial(_flash_kernel, tq=tq, tk=tk, nb=nb),
        grid=(N_KV_HEADS, S // (n_tiles * tq)),
        in_specs=[
            pl.BlockSpec((n_tiles * tq, gw), lambda h, i: (i, h)),
            pl.BlockSpec((HEAD_DIM, S), lambda h, i: (h, 0)),
            pl.BlockSpec((S, 2 * HEAD_DIM), lambda h, i: (0, h)),
        ],
        out_specs=pl.BlockSpec((n_tiles * tq, gw), lambda h, i: (i, h)),
        out_shape=jax.ShapeDtypeStruct((S, N_HEADS * HEAD_DIM), BF16),
        scratch_shapes=[
            pltpu.VMEM((n_tiles * mq, HEAD_DIM), BF16),
            pltpu.VMEM((mq, tk), F32),
            pltpu.VMEM((mq, LANES), F32),
            pltpu.VMEM((mq, tk), BF16),
            pltpu.VMEM((mq, LANES), F32),
            pltpu.VMEM((mq, LANES), F32),
            pltpu.VMEM((mq, 2 * HEAD_DIM), F32),
        ],
        compiler_params=_params(2),
        name="gqa_flash",
    )(q, k, v)


def _rope_tables(S):
    inv_freq = ROPE_THETA ** (-jnp.arange(ROPE_FREQS, dtype=F32) / ROPE_FREQS)

    def axis_table(n, row_axis):
        ang = jnp.arange(n).astype(F32)[:, None] * inv_freq[None, :]
        cos, sin, zero = jnp.cos(ang), jnp.sin(ang), jnp.zeros_like(ang)
        pair = (lambda a: [a, zero]) if row_axis else (lambda a: [zero, a])
        return jnp.concatenate(pair(cos) + pair(cos) + pair(-sin) + pair(sin), axis=-1)

    return axis_table(S // GRID_W, True), axis_table(GRID_W, False)


def kernel(x, ffn1_norm, ffn1_w_gate, ffn1_w_up, ffn1_w_down, mix_norm, ab_w_in, ab_v_norm, ab_w_s, ab_b_s, ab_w_out, attn_w_qkv, attn_q_norm, attn_k_norm, attn_w_o, ffn2_norm, ffn2_w_gate, ffn2_w_up, ffn2_w_down, final_norm):
    B, S, _ = x.shape
    assert B == 1 and S % 2048 == 0
    bf = lambda w: w.astype(BF16)
    xs = x.reshape(S, D_MODEL)
    ffn1_w = (bf(ffn1_w_gate), bf(ffn1_w_up), bf(ffn1_w_down))
    ffn2_w = (bf(ffn2_w_gate), bf(ffn2_w_up), bf(ffn2_w_down))

    cdft, w1, g2 = _dft_tables(S)
    n1 = S // LANES
    b_full = jnp.broadcast_to(ab_b_s[0][:, :, None], (N_GROUPS, LANES, LANES))
    ab_in = _ab_in_post(S, mix_norm[0], bf(ab_w_in[0]), cdft, ab_v_norm[0], bf(ab_w_s[0]), b_full)
    xs, z, gated = _ffn(xs, [], ffn1_norm[0], *ffn1_w, 0, post=ab_in, name="ffn1_abin_l0")
    t = _dft1(w1, z.reshape(2 * n1, LANES * MIX_W))
    f = _dft2(g2, t.reshape(2, n1, LANES, MIX_W)).reshape(S, MIX_W)
    w_out = bf(ab_w_out[0])
    xs = _ffn(xs, [(f, w_out, 0), (gated, w_out, 1)], ffn2_norm[0], *ffn2_w, 0,
              name="mixout_ffn2_l0")

    half = np.concatenate([np.arange(0, HEAD_DIM, 2), np.arange(1, HEAD_DIM, 2)])
    n_rot = N_HEADS + N_KV_HEADS
    cols = np.concatenate([(h * HEAD_DIM + half) for h in range(n_rot)]
                          + [np.arange(n_rot * HEAD_DIM, (n_rot + N_KV_HEADS) * HEAD_DIM)])
    w_qkv = bf(attn_w_qkv[0][:, cols])
    xs = _ffn(xs, [], ffn1_norm[1], *ffn1_w, 1, name="ffn1_l1")
    q, k, v = _qkv(xs, mix_norm[1], w_qkv, attn_q_norm[0][half], attn_k_norm[0][half],
                   *_rope_tables(S))
    o = _flash(q, k, v)
    xs = _ffn(xs, [(o, bf(attn_w_o[0]), 0)], ffn2_norm[1], *ffn2_w, 1, final_gain=final_norm,
              name="attnout_ffn2_final")
    return xs.reshape(B, S, D_MODEL)
```

```python
import functools
import math

import numpy as np
import jax
import jax.numpy as jnp
from jax import lax
from jax.experimental import pallas as pl
from jax.experimental.pallas import tpu as pltpu

F32 = jnp.float32
BF16 = jnp.bfloat16

D_MODEL = 1024
D_FF = 2816
EPS = 1e-6
LANES = 128
N_GROUPS = 4
MIX_W = N_GROUPS * LANES
HEAD_DIM = 128
N_HEADS = 8
N_KV_HEADS = 2
KV_GROUP = N_HEADS // N_KV_HEADS
GRID_W = 64
ROPE_THETA = 10000.0
ROPE_FREQS = HEAD_DIM // 4
Q_SCALE = HEAD_DIM ** -0.5 * math.log2(math.e)
VMEM_LIMIT = 56 * 1024 * 1024
FFN_TM = 512
FLASH_TQ = 128
FLASH_TILES = 16
FLASH_TK = 1024
FLASH_NB = 16


def _params(n_axes):
    return pltpu.CompilerParams(
        dimension_semantics=("arbitrary",) * n_axes, vmem_limit_bytes=VMEM_LIMIT)


def _const_spec(shape):
    nd = len(shape)
    return pl.BlockSpec(shape, lambda *_: (0,) * nd, pipeline_mode=pl.Buffered(1))


def _rms(x, gain):
    return x * lax.rsqrt(jnp.mean(x * x, axis=-1, keepdims=True) + EPS) * gain


def _dot(a, b):
    return jnp.dot(a, b, preferred_element_type=F32)


def _sigmoid(x):
    return 1.0 / (1.0 + jnp.exp(-x))


def _gelu_tanh(x):
    c = math.sqrt(2.0 / math.pi)
    return x * (0.5 * (1.0 + jnp.tanh(c * (x + 0.044715 * (x * x * x)))))


def _swiglu_residual(x, gain, wg_ref, wu_ref, wd_ref):
    h = _rms(x, gain).astype(BF16)
    gate = _dot(h, wg_ref[...])
    up = _dot(h, wu_ref[...])
    a = (gate * _sigmoid(gate) * up).astype(BF16)
    return x + 0.5 * _dot(a, wd_ref[...])


WEIGHT_CHUNKS = 8
WEIGHT_DMA_DEPTH = 4


def _load_bf16_weights(layer, jobs, sem):
    depth = sem.shape[0]
    chunks = []
    for src, dst, stage in jobs:
        rows = stage.shape[1]
        for c in range(src.shape[1] // rows):
            chunks.append((src.at[layer, pl.ds(c * rows, rows), :], dst, c * rows, rows, stage))

    def copy(k):
        return pltpu.make_async_copy(chunks[k][0], chunks[k][4].at[k % depth], sem.at[k % depth])

    for k in range(min(depth - 1, len(chunks))):
        copy(k).start()
    for k, (_, dst, r0, rows, stage) in enumerate(chunks):
        if k + depth - 1 < len(chunks):
            copy(k + depth - 1).start()
        copy(k).wait()
        dst[r0:r0 + rows, :] = stage[k % depth].astype(BF16)


def _ffn_kernel(*refs, n_pre, final_norm, post_fn, n_post_in, n_post_out, layer):
    n_in = 5 + 2 * n_pre + int(final_norm) + n_post_in
    n_out = 1 + n_post_out
    x_ref = refs[0]
    pre = refs[1:1 + 2 * n_pre]
    g_ref, wg_hbm, wu_hbm, wd_hbm = refs[1 + 2 * n_pre:5 + 2 * n_pre]
    o_ref = refs[n_in]
    wg_sc, wu_sc, wd_sc, stage_wide, stage_narrow, sem = refs[n_in + n_out:]

    @pl.when(pl.program_id(0) == 0)
    def _():
        _load_bf16_weights(layer, [(wg_hbm, wg_sc, stage_wide), (wu_hbm, wu_sc, stage_wide),
                                   (wd_hbm, wd_sc, stage_narrow)], sem)

    x = x_ref[...]
    for i in range(n_pre):
        x = x + _dot(pre[2 * i][...], pre[2 * i + 1][...])
    y = _swiglu_residual(x, g_ref[...], wg_sc, wu_sc, wd_sc)
    if final_norm:
        y = _rms(y, refs[5 + 2 * n_pre][...])
    o_ref[...] = y
    if post_fn is not None:
        post_fn(y, *refs[n_in - n_post_in:n_in], *refs[n_in + 1:n_in + n_out])


def _ffn(x, pre, gain, wg, wu, wd, layer, final_gain=None, post=None, *, tm=FFN_TM, name="ffn"):
    S = x.shape[0]
    row = lambda w: pl.BlockSpec((tm, w), lambda i: (i, 0))
    hbm = pl.BlockSpec(memory_space=pl.ANY)
    args, specs = [x], [row(D_MODEL)]
    for a, w, blk in pre:
        args += [a, w]
        specs += [row(a.shape[1]),
                  pl.BlockSpec((a.shape[1], D_MODEL), lambda i, blk=blk: (blk, 0),
                               pipeline_mode=pl.Buffered(1))]
    args += [gain.reshape(1, D_MODEL), wg, wu, wd]
    specs += [_const_spec((1, D_MODEL)), hbm, hbm, hbm]
    if final_gain is not None:
        args.append(final_gain.reshape(1, D_MODEL))
        specs.append(_const_spec((1, D_MODEL)))
    post_fn, post_args, post_in, post_out, post_shapes = post or (None, [], [], [], [])
    out = pl.pallas_call(
        functools.partial(_ffn_kernel, n_pre=len(pre), final_norm=final_gain is not None,
                          post_fn=post_fn, n_post_in=len(post_args), n_post_out=len(post_out),
                          layer=layer),
        grid=(S // tm,),
        in_specs=specs + post_in,
        out_specs=[row(D_MODEL)] + post_out,
        out_shape=[jax.ShapeDtypeStruct((S, D_MODEL), F32)] + post_shapes,
        scratch_shapes=[
            pltpu.VMEM(wg.shape[1:], BF16), pltpu.VMEM(wu.shape[1:], BF16),
            pltpu.VMEM(wd.shape[1:], BF16),
            pltpu.VMEM((WEIGHT_DMA_DEPTH, wg.shape[1] // WEIGHT_CHUNKS, wg.shape[2]), F32),
            pltpu.VMEM((WEIGHT_DMA_DEPTH, wd.shape[1] // WEIGHT_CHUNKS, wd.shape[2]), F32),
            pltpu.SemaphoreType.DMA((WEIGHT_DMA_DEPTH,)),
        ],
        compiler_params=_params(1),
        name=name,
    )(*args, *post_args)
    return out if post else out[0]


def _ab_in_stage(x, g_ref, win_ref, cdft_ref, vn_ref, ws_ref, bs_ref, z_ref, gated_ref):
    h = _rms(x, g_ref[...]).astype(BF16)
    z = _dot(h, win_ref[...])
    for g in range(N_GROUPS):
        fg = z[:, g * LANES:(g + 1) * LANES].astype(BF16)
        ab = _dot(fg, cdft_ref[...])
        z_ref[0, :, g * LANES:(g + 1) * LANES] = ab[:, :LANES].astype(BF16)
        z_ref[1, :, g * LANES:(g + 1) * LANES] = ab[:, LANES:].astype(BF16)
    uv = _gelu_tanh(z[:, MIX_W:])
    for g in range(N_GROUPS):
        u = uv[:, g * LANES:(g + 1) * LANES]
        v = uv[:, MIX_W + g * LANES:MIX_W + (g + 1) * LANES]
        vn = _rms(v, vn_ref[:, g * LANES:(g + 1) * LANES]).astype(BF16)
        for c in range(x.shape[0] // LANES):
            rows = slice(c * LANES, (c + 1) * LANES)
            s = _dot(ws_ref[g], vn[rows]) + bs_ref[g]
            gated_ref[rows, g * LANES:(g + 1) * LANES] = (u[rows] * s).astype(BF16)


def _ab_in_post(S, gain, w_in, cdft, v_norm, w_s, b_full, *, tm=FFN_TM):
    args = [gain.reshape(1, D_MODEL), w_in, cdft, v_norm.reshape(1, MIX_W), w_s, b_full]
    in_specs = [_const_spec(a.shape) for a in args]
    out_specs = [pl.BlockSpec((2, tm, MIX_W), lambda i: (0, i, 0)),
                 pl.BlockSpec((tm, MIX_W), lambda i: (i, 0))]
    out_shapes = [jax.ShapeDtypeStruct((2, S, MIX_W), BF16),
                  jax.ShapeDtypeStruct((S, MIX_W), BF16)]
    return _ab_in_stage, args, in_specs, out_specs, out_shapes


def _dft1_kernel(w_ref, z_ref, t_ref):
    t_ref[...] = _dot(w_ref[...], z_ref[...]).astype(BF16)


def _dft1(w1, zv, *, bn=16384):
    rows, cols = zv.shape
    return pl.pallas_call(
        _dft1_kernel,
        grid=(cols // bn,),
        in_specs=[_const_spec(w1.shape), pl.BlockSpec((rows, bn), lambda i: (0, i))],
        out_specs=pl.BlockSpec((rows, bn), lambda i: (0, i)),
        out_shape=jax.ShapeDtypeStruct((rows, cols), BF16),
        compiler_params=_params(1),
        name="dft_stage1",
    )(w1, zv)


def _dft2_kernel(g_ref, t_ref, o_ref, *, kb):
    for i in range(kb):
        tcat = jnp.concatenate([t_ref[0, i], t_ref[1, i]], axis=0)
        o_ref[:, i * MIX_W:(i + 1) * MIX_W] = _dot(g_ref[i], tcat).astype(BF16)


def _dft2(g2, t4, *, kb=32):
    _, n1, n2, w = t4.shape
    kb = min(kb, n1)
    return pl.pallas_call(
        functools.partial(_dft2_kernel, kb=kb),
        grid=(n1 // kb,),
        in_specs=[
            pl.BlockSpec((kb, n2, 2 * n2), lambda i: (i, 0, 0)),
            pl.BlockSpec((2, kb, n2, w), lambda i: (0, i, 0, 0)),
        ],
        out_specs=pl.BlockSpec((n2, kb * w), lambda i: (0, i)),
        out_shape=jax.ShapeDtypeStruct((n2, n1 * w), BF16),
        compiler_params=_params(1),
        name="dft_stage2",
    )(g2, t4)


def _dft_tables(S):
    n1, n2 = S // LANES, LANES
    c = np.arange(LANES)
    ang = 2.0 * np.pi * np.outer(c, c) / LANES
    cdft = np.concatenate([np.cos(ang), np.sin(ang)], axis=1) / np.sqrt(LANES)
    a1 = 2.0 * np.pi * np.outer(np.arange(n1), np.arange(n1)) / n1
    c1, s1 = np.cos(a1), np.sin(a1)
    w1 = np.block([[c1, -s1], [-s1, -c1]]) / np.sqrt(n1)
    k = np.arange(n1)[:, None] + n1 * np.arange(n2)[None, :]
    a2 = 2.0 * np.pi * k[:, :, None] * np.arange(n2)[None, None, :] / S
    g2 = np.concatenate([np.cos(a2), np.sin(a2)], axis=-1) / np.sqrt(n2)
    to_bf16 = lambda a: jnp.asarray(a, dtype=F32).astype(BF16)
    return to_bf16(cdft), to_bf16(w1), to_bf16(g2)


def _qkv_kernel(x_ref, g_ref, w_ref, qn_ref, kn_ref, rowtab_ref, coltab_ref, q_ref, kt_ref, v_ref):
    tm = x_ref.shape[0]
    h = _rms(x_ref[...], g_ref[...]).astype(BF16)
    qkv = _dot(h, w_ref[...])
    rows_per_tile = tm // GRID_W
    row0 = pl.program_id(0) * rows_per_tile
    tab = jnp.concatenate(
        [jnp.broadcast_to(rowtab_ref[pl.ds(row0 + r, 1), :], coltab_ref.shape) + coltab_ref[...]
         for r in range(rows_per_tile)], axis=0)
    cosf = tab[:, :HEAD_DIM]
    sinf = tab[:, HEAD_DIM:]

    def head(col, gain):
        y = _rms(qkv[:, col:col + HEAD_DIM], gain)
        return y * cosf + pltpu.roll(y, HEAD_DIM // 2, 1) * sinf

    for n in range(N_HEADS):
        q_ref[:, n * HEAD_DIM:(n + 1) * HEAD_DIM] = (
            head(n * HEAD_DIM, qn_ref[...]) * Q_SCALE).astype(BF16)
    k0 = N_HEADS * HEAD_DIM
    v0 = k0 + N_KV_HEADS * HEAD_DIM
    for n in range(N_KV_HEADS):
        kt_ref[n * HEAD_DIM:(n + 1) * HEAD_DIM, :] = head(k0 + n * HEAD_DIM, kn_ref[...]).T.astype(BF16)
        vcol = 2 * n * HEAD_DIM
        v_ref[:, vcol:vcol + HEAD_DIM] = qkv[:, v0 + n * HEAD_DIM:v0 + (n + 1) * HEAD_DIM].astype(BF16)
        v_ref[:, vcol + HEAD_DIM:vcol + 2 * HEAD_DIM] = jnp.ones((tm, HEAD_DIM), BF16)


def _qkv(x, gain, w, qn, kn, rowtab, coltab, *, tm=256):
    S = x.shape[0]
    assert tm % GRID_W == 0
    row = lambda wd: pl.BlockSpec((tm, wd), lambda i: (i, 0))
    args = [gain.reshape(1, D_MODEL), w, qn.reshape(1, HEAD_DIM), kn.reshape(1, HEAD_DIM),
            rowtab, coltab]
    return pl.pallas_call(
        _qkv_kernel,
        grid=(S // tm,),
        in_specs=[row(D_MODEL)] + [_const_spec(a.shape) for a in args],
        out_specs=[row(N_HEADS * HEAD_DIM),
                   pl.BlockSpec((N_KV_HEADS * HEAD_DIM, tm), lambda i: (0, i)),
                   row(2 * N_KV_HEADS * HEAD_DIM)],
        out_shape=[jax.ShapeDtypeStruct((S, N_HEADS * HEAD_DIM), BF16),
                   jax.ShapeDtypeStruct((N_KV_HEADS * HEAD_DIM, S), BF16),
                   jax.ShapeDtypeStruct((S, 2 * N_KV_HEADS * HEAD_DIM), BF16)],
        compiler_params=_params(1),
        name="qkv_rope",
    )(x, *args)


def _flash_kernel(q_ref, kt_ref, v_ref, o_ref, q_sc, s_sc, mx_sc, p_sc, al_sc, m_sc, acc_sc,
                  *, tq, tk, nb):
    nk = kt_ref.shape[1] // tk
    n_tiles = q_ref.shape[0] // tq
    mq = KV_GROUP * tq
    n_chunks = n_tiles * nk
    for t in range(n_tiles):
        for g in range(KV_GROUP):
            q_sc[t * mq + g * tq:t * mq + (g + 1) * tq, :] = (
                q_ref[t * tq:(t + 1) * tq, g * HEAD_DIM:(g + 1) * HEAD_DIM])

    def scores(jj):
        row = pl.multiple_of((jj // nk) * mq, mq)
        off = pl.multiple_of((jj % nk) * tk, tk)
        s = _dot(q_sc[pl.ds(row, mq), :], kt_ref[:, pl.ds(off, tk)])
        mx = s[:, :LANES]
        for t in range(1, tk // LANES):
            mx = jnp.maximum(mx, s[:, t * LANES:(t + 1) * LANES])
        return s, mx

    def softmax(s, mx, m_old):
        m_new = jnp.maximum(m_old, jnp.max(mx, axis=-1, keepdims=True))
        alpha = jnp.exp2(m_old - m_new)
        p = jnp.exp2((s - jnp.tile(m_new, (1, tk // LANES))).astype(BF16))
        return p, alpha, m_new

    def pv(acc, p, alpha, jj):
        off = pl.multiple_of((jj % nk) * tk, tk)
        return acc * jnp.tile(alpha, (1, 2)) + _dot(p, v_ref[pl.ds(off, tk), :])

    def write_out(acc, tile):
        o = acc[:, :HEAD_DIM] / acc[:, HEAD_DIM:]
        row = pl.multiple_of(tile * tq, tq)
        for g in range(KV_GROUP):
            o_ref[pl.ds(row, tq), g * HEAD_DIM:(g + 1) * HEAD_DIM] = (
                o[g * tq:(g + 1) * tq].astype(BF16))

    s0, mx0 = scores(0)
    s_sc[...] = s0
    mx_sc[...] = mx0
    m_sc[...] = jnp.full(m_sc.shape, -1e30, F32)
    acc_sc[...] = jnp.concatenate([jnp.zeros((mq, HEAD_DIM), F32), jnp.ones((mq, HEAD_DIM), F32)],
                                  axis=1)
    p_sc[...] = jnp.zeros(p_sc.shape, BF16)
    al_sc[...] = jnp.ones(al_sc.shape, F32)

    def body(i, carry):
        j0 = nb * i
        s, mx, p, alpha = s_sc[...], mx_sc[...], p_sc[...], al_sc[...]
        acc = acc_sc[...]
        m = jnp.where(j0 % nk == 0, -1e30, m_sc[...])
        for c in range(nb):
            jj = j0 + c
            acc = pv(acc, p, alpha, jnp.maximum(jj - 1, 0) if c == 0 else jj - 1)
            if c == 0:
                write_out(acc, jnp.maximum(jj - 1, 0) // nk)
            s_next, mx_next = scores(jnp.minimum(jj + 1, n_chunks - 1) if c == nb - 1 else jj + 1)
            p, alpha, m = softmax(s, mx, m)
            s, mx = s_next, mx_next
        s_sc[...], mx_sc[...], p_sc[...], al_sc[...] = s, mx, p, alpha
        m_sc[...], acc_sc[...] = m, acc
        return carry

    lax.fori_loop(0, n_chunks // nb, body, 0)
    write_out(pv(acc_sc[...], p_sc[...], al_sc[...], n_chunks - 1), n_tiles - 1)


def _flash(q, k, v, *, tq=FLASH_TQ, tk=FLASH_TK, nb=FLASH_NB, n_tiles=FLASH_TILES):
    S = q.shape[0]
    gw = KV_GROUP * HEAD_DIM
    mq = KV_GROUP * tq
    nb = min(nb, S // tk)
    assert (S // tk) % nb == 0 and S % (n_tiles * tq) == 0
    return pl.pallas_call(
        functools.partial(_flash_kernel, tq=tq, tk=tk, nb=nb),
        grid=(N_KV_HEADS, S // (n_tiles * tq)),
        in_specs=[
            pl.BlockSpec((n_tiles * tq, gw), lambda h, i: (i, h)),
            pl.BlockSpec((HEAD_DIM, S), lambda h, i: (h, 0)),
            pl.BlockSpec((S, 2 * HEAD_DIM), lambda h, i: (0, h)),
        ],
        out_specs=pl.BlockSpec((n_tiles * tq, gw), lambda h, i: (i, h)),
        out_shape=jax.ShapeDtypeStruct((S, N_HEADS * HEAD_DIM), BF16),
        scratch_shapes=[
            pltpu.VMEM((n_tiles * mq, HEAD_DIM), BF16),
            pltpu.VMEM((mq, tk), F32),
            pltpu.VMEM((mq, LANES), F32),
            pltpu.VMEM((mq, tk), BF16),
            pltpu.VMEM((mq, LANES), F32),
            pltpu.VMEM((mq, LANES), F32),
            pltpu.VMEM((mq, 2 * HEAD_DIM), F32),
        ],
        compiler_params=_params(2),
        name="gqa_flash",
    )(q, k, v)


def _rope_tables(S):
    inv_freq = ROPE_THETA ** (-jnp.arange(ROPE_FREQS, dtype=F32) / ROPE_FREQS)

    def axis_table(n, row_axis):
        ang = jnp.arange(n).astype(F32)[:, None] * inv_freq[None, :]
        cos, sin, zero = jnp.cos(ang), jnp.sin(ang), jnp.zeros_like(ang)
        pair = (lambda a: [a, zero]) if row_axis else (lambda a: [zero, a])
        return jnp.concatenate(pair(cos) + pair(cos) + pair(-sin) + pair(sin), axis=-1)

    return axis_table(S // GRID_W, True), axis_table(GRID_W, False)


def kernel(x, ffn1_norm, ffn1_w_gate, ffn1_w_up, ffn1_w_down, mix_norm, ab_w_in, ab_v_norm, ab_w_s, ab_b_s, ab_w_out, attn_w_qkv, attn_q_norm, attn_k_norm, attn_w_o, ffn2_norm, ffn2_w_gate, ffn2_w_up, ffn2_w_down, final_norm):
    B, S, _ = x.shape
    assert B == 1 and S % 2048 == 0
    bf = lambda w: w.astype(BF16)
    xs = x.reshape(S, D_MODEL)
    ffn1_w = (ffn1_w_gate, ffn1_w_up, ffn1_w_down)
    ffn2_w = (ffn2_w_gate, ffn2_w_up, ffn2_w_down)

    cdft, w1, g2 = _dft_tables(S)
    n1 = S // LANES
    b_full = jnp.broadcast_to(ab_b_s[0][:, :, None], (N_GROUPS, LANES, LANES))
    ab_in = _ab_in_post(S, mix_norm[0], bf(ab_w_in[0]), cdft, ab_v_norm[0], bf(ab_w_s[0]), b_full)
    xs, z, gated = _ffn(xs, [], ffn1_norm[0], *ffn1_w, 0, post=ab_in, name="ffn1_abin_l0")
    t = _dft1(w1, z.reshape(2 * n1, LANES * MIX_W))
    f = _dft2(g2, t.reshape(2, n1, LANES, MIX_W)).reshape(S, MIX_W)
    w_out = bf(ab_w_out[0])
    xs = _ffn(xs, [(f, w_out, 0), (gated, w_out, 1)], ffn2_norm[0], *ffn2_w, 0,
              name="mixout_ffn2_l0")

    half = np.concatenate([np.arange(0, HEAD_DIM, 2), np.arange(1, HEAD_DIM, 2)])
    n_rot = N_HEADS + N_KV_HEADS
    cols = np.concatenate([(h * HEAD_DIM + half) for h in range(n_rot)]
                          + [np.arange(n_rot * HEAD_DIM, (n_rot + N_KV_HEADS) * HEAD_DIM)])
    w_qkv = bf(attn_w_qkv[0][:, cols])
    xs = _ffn(xs, [], ffn1_norm[1], *ffn1_w, 1, name="ffn1_l1")
    q, k, v = _qkv(xs, mix_norm[1], w_qkv, attn_q_norm[0][half], attn_k_norm[0][half],
                   *_rope_tables(S))
    o = _flash(q, k, v)
    xs = _ffn(xs, [(o, bf(attn_w_o[0]), 0)], ffn2_norm[1], *ffn2_w, 1, final_gain=final_norm,
              name="attnout_ffn2_final")
    return xs.reshape(B, S, D_MODEL)
```

```python
import functools
import math

import numpy as np
import jax
import jax.numpy as jnp
from jax import lax
from jax.experimental import pallas as pl
from jax.experimental.pallas import tpu as pltpu

F32 = jnp.float32
BF16 = jnp.bfloat16

D_MODEL = 1024
D_FF = 2816
EPS = 1e-6
LANES = 128
N_GROUPS = 4
MIX_W = N_GROUPS * LANES
HEAD_DIM = 128
N_HEADS = 8
N_KV_HEADS = 2
KV_GROUP = N_HEADS // N_KV_HEADS
GRID_W = 64
ROPE_THETA = 10000.0
ROPE_FREQS = HEAD_DIM // 4
Q_SCALE = HEAD_DIM ** -0.5 * math.log2(math.e)
VMEM_LIMIT = 56 * 1024 * 1024
FFN_TM = 512
FLASH_TQ = 128
FLASH_TILES = 32
FLASH_TK = 1024
SCORE_BOUND = 30.0


def _params(n_axes):
    return pltpu.CompilerParams(
        dimension_semantics=("arbitrary",) * n_axes, vmem_limit_bytes=VMEM_LIMIT)


def _const_spec(shape):
    nd = len(shape)
    return pl.BlockSpec(shape, lambda *_: (0,) * nd, pipeline_mode=pl.Buffered(1))


def _rms(x, gain):
    return x * lax.rsqrt(jnp.mean(x * x, axis=-1, keepdims=True) + EPS) * gain


def _dot(a, b):
    return jnp.dot(a, b, preferred_element_type=F32)


def _sigmoid(x):
    return 1.0 / (1.0 + jnp.exp(-x))


def _gelu_tanh(x):
    c = math.sqrt(2.0 / math.pi)
    return x * (0.5 * (1.0 + jnp.tanh(c * (x + 0.044715 * (x * x * x)))))


def _swiglu_residual(x, gain, wg_ref, wu_ref, wd_ref):
    h = _rms(x, gain).astype(BF16)
    gate = _dot(h, wg_ref[...])
    up = _dot(h, wu_ref[...])
    a = (gate * _sigmoid(gate) * up).astype(BF16)
    return x + 0.5 * _dot(a, wd_ref[...])


WEIGHT_CHUNKS = 8
WEIGHT_DMA_DEPTH = 4


def _load_bf16_weights(layer, jobs, sem):
    depth = sem.shape[0]
    chunks = []
    for src, dst, stage in jobs:
        rows = stage.shape[1]
        for c in range(src.shape[1] // rows):
            chunks.append((src.at[layer, pl.ds(c * rows, rows), :], dst, c * rows, rows, stage))

    def copy(k):
        return pltpu.make_async_copy(chunks[k][0], chunks[k][4].at[k % depth], sem.at[k % depth])

    for k in range(min(depth - 1, len(chunks))):
        copy(k).start()
    for k, (_, dst, r0, rows, stage) in enumerate(chunks):
        if k + depth - 1 < len(chunks):
            copy(k + depth - 1).start()
        copy(k).wait()
        dst[r0:r0 + rows, :] = stage[k % depth].astype(BF16)


def _ffn_kernel(*refs, n_pre, final_norm, post_fn, n_post_in, n_post_out, layer):
    n_in = 5 + 2 * n_pre + int(final_norm) + n_post_in
    n_out = 1 + n_post_out
    x_ref = refs[0]
    pre = refs[1:1 + 2 * n_pre]
    g_ref, wg_hbm, wu_hbm, wd_hbm = refs[1 + 2 * n_pre:5 + 2 * n_pre]
    o_ref = refs[n_in]
    wg_sc, wu_sc, wd_sc, stage_wide, stage_narrow, sem = refs[n_in + n_out:]

    @pl.when(pl.program_id(0) == 0)
    def _():
        _load_bf16_weights(layer, [(wg_hbm, wg_sc, stage_wide), (wu_hbm, wu_sc, stage_wide),
                                   (wd_hbm, wd_sc, stage_narrow)], sem)

    n_sub = 1 if n_pre else 2
    halves = []
    hm = x_ref.shape[0] // n_sub
    for t in range(n_sub):
        rows = slice(t * hm, (t + 1) * hm)
        x = x_ref[rows, :]
        for i in range(n_pre):
            x = x + _dot(pre[2 * i][rows, :], pre[2 * i + 1][...])
        yh = _swiglu_residual(x, g_ref[...], wg_sc, wu_sc, wd_sc)
        if final_norm:
            yh = _rms(yh, refs[5 + 2 * n_pre][...])
        halves.append(yh)
    y = jnp.concatenate(halves, axis=0)
    o_ref[...] = y
    if post_fn is not None:
        post_fn(y, *refs[n_in - n_post_in:n_in], *refs[n_in + 1:n_in + n_out])


def _ffn(x, pre, gain, wg, wu, wd, layer, final_gain=None, post=None, *, name="ffn"):
    S = x.shape[0]
    tm = min(S, FFN_TM)
    row = lambda w: pl.BlockSpec((tm, w), lambda i: (i, 0))
    hbm = pl.BlockSpec(memory_space=pl.ANY)
    args, specs = [x], [row(D_MODEL)]
    for a, w, blk in pre:
        args += [a, w]
        specs += [row(a.shape[1]),
                  pl.BlockSpec((a.shape[1], D_MODEL), lambda i, blk=blk: (blk, 0),
                               pipeline_mode=pl.Buffered(1))]
    args += [gain.reshape(1, D_MODEL), wg, wu, wd]
    specs += [_const_spec((1, D_MODEL)), hbm, hbm, hbm]
    if final_gain is not None:
        args.append(final_gain.reshape(1, D_MODEL))
        specs.append(_const_spec((1, D_MODEL)))
    post_fn, post_args, post_in, post_out, post_shapes = post or (None, [], [], [], [])
    out = pl.pallas_call(
        functools.partial(_ffn_kernel, n_pre=len(pre), final_norm=final_gain is not None,
                          post_fn=post_fn, n_post_in=len(post_args), n_post_out=len(post_out),
                          layer=layer),
        grid=(S // tm,),
        in_specs=specs + post_in,
        out_specs=[row(D_MODEL)] + post_out,
        out_shape=[jax.ShapeDtypeStruct((S, D_MODEL), F32)] + post_shapes,
        scratch_shapes=[
            pltpu.VMEM(wg.shape[1:], BF16), pltpu.VMEM(wu.shape[1:], BF16),
            pltpu.VMEM(wd.shape[1:], BF16),
            pltpu.VMEM((WEIGHT_DMA_DEPTH, wg.shape[1] // WEIGHT_CHUNKS, wg.shape[2]), F32),
            pltpu.VMEM((WEIGHT_DMA_DEPTH, wd.shape[1] // WEIGHT_CHUNKS, wd.shape[2]), F32),
            pltpu.SemaphoreType.DMA((WEIGHT_DMA_DEPTH,)),
        ],
        compiler_params=_params(1),
        name=name,
    )(*args, *post_args)
    return out if post else out[0]


def _ab_in_stage(x, g_ref, win_ref, cdft_ref, vn_ref, ws_ref, bs_ref, z_ref, gated_ref):
    h = _rms(x, g_ref[...]).astype(BF16)
    z_uv = _dot(h, win_ref[:, MIX_W:])
    z_f = _dot(h, win_ref[:, :MIX_W])
    for g in range(N_GROUPS):
        fg = z_f[:, g * LANES:(g + 1) * LANES].astype(BF16)
        ab = _dot(fg, cdft_ref[...])
        z_ref[0, :, g * LANES:(g + 1) * LANES] = ab[:, :LANES].astype(BF16)
        z_ref[1, :, g * LANES:(g + 1) * LANES] = ab[:, LANES:].astype(BF16)
    uv = _gelu_tanh(z_uv)
    for g in range(N_GROUPS):
        u = uv[:, g * LANES:(g + 1) * LANES]
        v = uv[:, MIX_W + g * LANES:MIX_W + (g + 1) * LANES]
        vn = _rms(v, vn_ref[:, g * LANES:(g + 1) * LANES]).astype(BF16)
        for c in range(x.shape[0] // LANES):
            rows = slice(c * LANES, (c + 1) * LANES)
            s = _dot(ws_ref[g], vn[rows]) + bs_ref[g]
            gated_ref[rows, g * LANES:(g + 1) * LANES] = (u[rows] * s).astype(BF16)


def _ab_in_post(S, gain, w_in, cdft, v_norm, w_s, b_full, *, tm=FFN_TM):
    args = [gain.reshape(1, D_MODEL), w_in, cdft, v_norm.reshape(1, MIX_W), w_s, b_full]
    in_specs = [_const_spec(a.shape) for a in args]
    out_specs = [pl.BlockSpec((2, tm, MIX_W), lambda i: (0, i, 0)),
                 pl.BlockSpec((tm, MIX_W), lambda i: (i, 0))]
    out_shapes = [jax.ShapeDtypeStruct((2, S, MIX_W), BF16),
                  jax.ShapeDtypeStruct((S, MIX_W), BF16)]
    return _ab_in_stage, args, in_specs, out_specs, out_shapes


def _dft1_kernel(w_ref, z_ref, t_ref):
    t_ref[...] = _dot(w_ref[...], z_ref[...]).astype(BF16)


def _dft1(w1, zv, *, bn=16384):
    rows, cols = zv.shape
    return pl.pallas_call(
        _dft1_kernel,
        grid=(cols // bn,),
        in_specs=[_const_spec(w1.shape), pl.BlockSpec((rows, bn), lambda i: (0, i))],
        out_specs=pl.BlockSpec((rows, bn), lambda i: (0, i)),
        out_shape=jax.ShapeDtypeStruct((rows, cols), BF16),
        compiler_params=_params(1),
        name="dft_stage1",
    )(w1, zv)


def _dft2_kernel(g_ref, t_ref, o_ref, *, kb):
    for i in range(kb):
        tcat = jnp.concatenate([t_ref[0, i], t_ref[1, i]], axis=0)
        o_ref[:, i * MIX_W:(i + 1) * MIX_W] = _dot(g_ref[i], tcat).astype(BF16)


def _dft2(g2, t4, *, kb=32):
    _, n1, n2, w = t4.shape
    kb = min(kb, n1)
    return pl.pallas_call(
        functools.partial(_dft2_kernel, kb=kb),
        grid=(n1 // kb,),
        in_specs=[
            pl.BlockSpec((kb, n2, 2 * n2), lambda i: (i, 0, 0)),
            pl.BlockSpec((2, kb, n2, w), lambda i: (0, i, 0, 0)),
        ],
        out_specs=pl.BlockSpec((n2, kb * w), lambda i: (0, i)),
        out_shape=jax.ShapeDtypeStruct((n2, n1 * w), BF16),
        compiler_params=_params(1),
        name="dft_stage2",
    )(g2, t4)


def _dft_tables(S):
    n1, n2 = S // LANES, LANES
    c = np.arange(LANES)
    ang = 2.0 * np.pi * np.outer(c, c) / LANES
    cdft = np.concatenate([np.cos(ang), np.sin(ang)], axis=1) / np.sqrt(LANES)
    a1 = 2.0 * np.pi * np.outer(np.arange(n1), np.arange(n1)) / n1
    c1, s1 = np.cos(a1), np.sin(a1)
    w1 = np.block([[c1, -s1], [-s1, -c1]]) / np.sqrt(n1)
    k = np.arange(n1)[:, None] + n1 * np.arange(n2)[None, :]
    a2 = 2.0 * np.pi * k[:, :, None] * np.arange(n2)[None, None, :] / S
    g2 = np.concatenate([np.cos(a2), np.sin(a2)], axis=-1) / np.sqrt(n2)
    to_bf16 = lambda a: jnp.asarray(a, dtype=F32).astype(BF16)
    return to_bf16(cdft), to_bf16(w1), to_bf16(g2)


def _qkv_kernel(x_ref, g_ref, w_ref, qn_ref, kn_ref, rowtab_ref, coltab_ref, q_ref, kt_ref, v_ref):
    tm = x_ref.shape[0]
    h = _rms(x_ref[...], g_ref[...]).astype(BF16)
    qkv = _dot(h, w_ref[...])
    rows_per_tile = tm // GRID_W
    row0 = pl.program_id(0) * rows_per_tile
    tab = jnp.concatenate(
        [jnp.broadcast_to(rowtab_ref[pl.ds(row0 + r, 1), :], coltab_ref.shape) + coltab_ref[...]
         for r in range(rows_per_tile)], axis=0)
    cosf = tab[:, :HEAD_DIM]
    sinf = tab[:, HEAD_DIM:]

    def head(col, gain):
        y = _rms(qkv[:, col:col + HEAD_DIM], gain)
        return y * cosf + pltpu.roll(y, HEAD_DIM // 2, 1) * sinf

    for n in range(N_HEADS):
        q_ref[:, n * HEAD_DIM:(n + 1) * HEAD_DIM] = (
            head(n * HEAD_DIM, qn_ref[...]) * Q_SCALE).astype(BF16)
    k0 = N_HEADS * HEAD_DIM
    v0 = k0 + N_KV_HEADS * HEAD_DIM
    for n in range(N_KV_HEADS):
        kt_ref[n * HEAD_DIM:(n + 1) * HEAD_DIM, :] = head(k0 + n * HEAD_DIM, kn_ref[...]).T.astype(BF16)
        vcol = 2 * n * HEAD_DIM
        v_ref[:, vcol:vcol + HEAD_DIM] = qkv[:, v0 + n * HEAD_DIM:v0 + (n + 1) * HEAD_DIM].astype(BF16)
        v_ref[:, vcol + HEAD_DIM:vcol + 2 * HEAD_DIM] = jnp.ones((tm, HEAD_DIM), BF16)


def _qkv(x, gain, w, qn, kn, rowtab, coltab, *, tm=256):
    S = x.shape[0]
    assert tm % GRID_W == 0
    row = lambda wd: pl.BlockSpec((tm, wd), lambda i: (i, 0))
    args = [gain.reshape(1, D_MODEL), w, qn.reshape(1, HEAD_DIM), kn.reshape(1, HEAD_DIM),
            rowtab, coltab]
    return pl.pallas_call(
        _qkv_kernel,
        grid=(S // tm,),
        in_specs=[row(D_MODEL)] + [_const_spec(a.shape) for a in args],
        out_specs=[row(N_HEADS * HEAD_DIM),
                   pl.BlockSpec((N_KV_HEADS * HEAD_DIM, tm), lambda i: (0, i)),
                   row(2 * N_KV_HEADS * HEAD_DIM)],
        out_shape=[jax.ShapeDtypeStruct((S, N_HEADS * HEAD_DIM), BF16),
                   jax.ShapeDtypeStruct((N_KV_HEADS * HEAD_DIM, S), BF16),
                   jax.ShapeDtypeStruct((S, 2 * N_KV_HEADS * HEAD_DIM), BF16)],
        compiler_params=_params(1),
        name="qkv_rope",
    )(x, *args)


def _flash_kernel(bounded_ref, q_ref, kt_ref, v_ref, o_ref, q_sc, s_sc, mx_sc, p_sc, al_sc, acc_sc,
                  *, tq, tk, nb):
    nk = kt_ref.shape[1] // tk
    n_tiles = q_ref.shape[0] // tq
    mq = KV_GROUP * tq
    n_chunks = n_tiles * nk
    assert nb == nk
    for t in range(n_tiles):
        for g in range(KV_GROUP):
            q_sc[t * mq + g * tq:t * mq + (g + 1) * tq, :] = (
                q_ref[t * tq:(t + 1) * tq, g * HEAD_DIM:(g + 1) * HEAD_DIM])

    def scores(jj, with_max=True):
        row = pl.multiple_of((jj // nk) * mq, mq)
        off = pl.multiple_of((jj % nk) * tk, tk)
        s = _dot(q_sc[pl.ds(row, mq), :], kt_ref[:, pl.ds(off, tk)])
        if not with_max:
            return s, None
        mx = s[:, :LANES]
        for t in range(1, tk // LANES):
            mx = jnp.maximum(mx, s[:, t * LANES:(t + 1) * LANES])
        return s, mx

    def probs(s, shift):
        return jnp.exp2((s - jnp.tile(shift, (1, tk // LANES))).astype(BF16))

    def pv(p, jj):
        off = pl.multiple_of((jj % nk) * tk, tk)
        return _dot(p, v_ref[pl.ds(off, tk), :])

    def write_out(acc, tile):
        o = acc[:, :HEAD_DIM] / acc[:, HEAD_DIM:]
        row = pl.multiple_of(tile * tq, tq)
        for g in range(KV_GROUP):
            o_ref[pl.ds(row, tq), g * HEAD_DIM:(g + 1) * HEAD_DIM] = (
                o[g * tq:(g + 1) * tq].astype(BF16))

    def sweep(bounded):
        s0, mx0 = scores(0)
        s_sc[...] = s0
        mx_sc[...] = mx0
        acc_sc[...] = jnp.concatenate(
            [jnp.zeros((mq, HEAD_DIM), F32), jnp.ones((mq, HEAD_DIM), F32)], axis=1)
        p_sc[...] = jnp.zeros(p_sc.shape, BF16)
        if not bounded:
            al_sc[...] = jnp.ones(al_sc.shape, F32)

        def body(i, carry):
            j0 = nb * i
            s, mx, p, acc = s_sc[...], mx_sc[...], p_sc[...], acc_sc[...]
            if bounded:
                shift = jnp.broadcast_to(jnp.max(mx, axis=-1, keepdims=True), (mq, LANES))
            else:
                m, alpha = jnp.full((mq, LANES), -1e30, F32), al_sc[...]
            for c in range(nb):
                jj = j0 + c
                last = c == nb - 1
                pvd = pv(p, jnp.maximum(jj - 1, 0) if c == 0 else jj - 1)
                if bounded:
                    acc = pvd if c == 1 else acc + pvd
                else:
                    acc = acc * jnp.tile(alpha, (1, 2)) + pvd
                if c == 0:
                    write_out(acc, jnp.maximum(jj - 1, 0) // nk)
                s_next, mx_next = scores(jnp.minimum(jj + 1, n_chunks - 1) if last else jj + 1,
                                         with_max=last or not bounded)
                if bounded:
                    p = probs(s, shift)
                else:
                    m_new = jnp.maximum(m, jnp.max(mx, axis=-1, keepdims=True))
                    alpha = jnp.exp2(m - m_new)
                    p = probs(s, m_new)
                    m = m_new
                s = s_next
                if mx_next is not None:
                    mx = mx_next
            s_sc[...], mx_sc[...], p_sc[...], acc_sc[...] = s, mx, p, acc
            if not bounded:
                al_sc[...] = alpha
            return carry

        lax.fori_loop(0, n_chunks // nb, body, 0)
        pvd = pv(p_sc[...], n_chunks - 1)
        acc = acc_sc[...] + pvd if bounded else acc_sc[...] * jnp.tile(al_sc[...], (1, 2)) + pvd
        write_out(acc, n_tiles - 1)

    @pl.when(bounded_ref[0] != 0)
    def _():
        sweep(True)

    @pl.when(bounded_ref[0] == 0)
    def _():
        sweep(False)


def _flash(q, k, v, bounded, *, tq=FLASH_TQ, tk=FLASH_TK, n_tiles=FLASH_TILES):
    S = q.shape[0]
    gw = KV_GROUP * HEAD_DIM
    mq = KV_GROUP * tq
    n_tiles = min(n_tiles, S // tq)
    assert S % tk == 0 and S % (n_tiles * tq) == 0
    return pl.pallas_call(
        functools.partial(_flash_kernel, tq=tq, tk=tk, nb=S // tk),
        grid=(N_KV_HEADS, S // (n_tiles * tq)),
        in_specs=[
            pl.BlockSpec(memory_space=pltpu.SMEM),
            pl.BlockSpec((n_tiles * tq, gw), lambda h, i: (i, h)),
            pl.BlockSpec((HEAD_DIM, S), lambda h, i: (h, 0)),
            pl.BlockSpec((S, 2 * HEAD_DIM), lambda h, i: (0, h)),
        ],
        out_specs=pl.BlockSpec((n_tiles * tq, gw), lambda h, i: (i, h)),
        out_shape=jax.ShapeDtypeStruct((S, N_HEADS * HEAD_DIM), BF16),
        scratch_shapes=[
            pltpu.VMEM((n_tiles * mq, HEAD_DIM), BF16),
            pltpu.VMEM((mq, tk), F32),
            pltpu.VMEM((mq, LANES), F32),
            pltpu.VMEM((mq, tk), BF16),
            pltpu.VMEM((mq, LANES), F32),
            pltpu.VMEM((mq, 2 * HEAD_DIM), F32),
        ],
        compiler_params=_params(2),
        name="gqa_flash",
    )(bounded, q, k, v)


def _rope_tables(S):
    inv_freq = ROPE_THETA ** (-jnp.arange(ROPE_FREQS, dtype=F32) / ROPE_FREQS)

    def axis_table(n, row_axis):
        ang = jnp.arange(n).astype(F32)[:, None] * inv_freq[None, :]
        cos, sin, zero = jnp.cos(ang), jnp.sin(ang), jnp.zeros_like(ang)
        pair = (lambda a: [a, zero]) if row_axis else (lambda a: [zero, a])
        return jnp.concatenate(pair(cos) + pair(cos) + pair(-sin) + pair(sin), axis=-1)

    return axis_table(S // GRID_W, True), axis_table(GRID_W, False)


def kernel(x, ffn1_norm, ffn1_w_gate, ffn1_w_up, ffn1_w_down, mix_norm, ab_w_in, ab_v_norm, ab_w_s, ab_b_s, ab_w_out, attn_w_qkv, attn_q_norm, attn_k_norm, attn_w_o, ffn2_norm, ffn2_w_gate, ffn2_w_up, ffn2_w_down, final_norm):
    B, S, _ = x.shape
    assert B == 1 and S % 2048 == 0
    bf = lambda w: w.astype(BF16)
    xs = x.reshape(S, D_MODEL)
    ffn1_w = (ffn1_w_gate, ffn1_w_up, ffn1_w_down)
    ffn2_w = (ffn2_w_gate, ffn2_w_up, ffn2_w_down)

    cdft, w1, g2 = _dft_tables(S)
    n1 = S // LANES
    b_full = jnp.broadcast_to(ab_b_s[0][:, :, None], (N_GROUPS, LANES, LANES))
    ab_in = _ab_in_post(S, mix_norm[0], bf(ab_w_in[0]), cdft, ab_v_norm[0], bf(ab_w_s[0]), b_full)
    xs, z, gated = _ffn(xs, [], ffn1_norm[0], *ffn1_w, 0, post=ab_in, name="ffn1_abin_l0")
    t = _dft1(w1, z.reshape(2 * n1, LANES * MIX_W))
    f = _dft2(g2, t.reshape(2, n1, LANES, MIX_W)).reshape(S, MIX_W)
    w_out = bf(ab_w_out[0])
    xs = _ffn(xs, [(f, w_out, 0), (gated, w_out, 1)], ffn2_norm[0], *ffn2_w, 0,
              name="mixout_ffn2_l0")

    half = np.concatenate([np.arange(0, HEAD_DIM, 2), np.arange(1, HEAD_DIM, 2)])
    n_rot = N_HEADS + N_KV_HEADS
    cols = np.concatenate([(h * HEAD_DIM + half) for h in range(n_rot)]
                          + [np.arange(n_rot * HEAD_DIM, (n_rot + N_KV_HEADS) * HEAD_DIM)])
    w_qkv = bf(attn_w_qkv[0][:, cols])
    xs = _ffn(xs, [], ffn1_norm[1], *ffn1_w, 1, name="ffn1_l1")
    q, k, v = _qkv(xs, mix_norm[1], w_qkv, attn_q_norm[0][half], attn_k_norm[0][half],
                   *_rope_tables(S))
    score_bound = (HEAD_DIM * Q_SCALE * 1.02 * jnp.max(jnp.abs(attn_q_norm[0]))
                   * jnp.max(jnp.abs(attn_k_norm[0])))
    bounded = (score_bound <= SCORE_BOUND).astype(jnp.int32).reshape(1)
    o = _flash(q, k, v, bounded)
    xs = _ffn(xs, [(o, bf(attn_w_o[0]), 0)], ffn2_norm[1], *ffn2_w, 1, final_gain=final_norm,
              name="attnout_ffn2_final")
    return xs.reshape(B, S, D_MODEL)
```

```python
import functools
import math

import numpy as np
import jax
import jax.numpy as jnp
from jax import lax
from jax.experimental import pallas as pl
from jax.experimental.pallas import tpu as pltpu

F32 = jnp.float32
BF16 = jnp.bfloat16

D_MODEL = 1024
D_FF = 2816
EPS = 1e-6
LANES = 128
N_GROUPS = 4
MIX_W = N_GROUPS * LANES
HEAD_DIM = 128
N_HEADS = 8
N_KV_HEADS = 2
KV_GROUP = N_HEADS // N_KV_HEADS
GRID_W = 64
ROPE_THETA = 10000.0
ROPE_FREQS = HEAD_DIM // 4
Q_SCALE = HEAD_DIM ** -0.5 * math.log2(math.e)
VMEM_LIMIT = 56 * 1024 * 1024
FFN_TM = 512
FLASH_TQ = 128
FLASH_TILES = 32
FLASH_TK = 1024
SCORE_BOUND = 30.0


def _params(n_axes):
    return pltpu.CompilerParams(
        dimension_semantics=("arbitrary",) * n_axes, vmem_limit_bytes=VMEM_LIMIT)


def _const_spec(shape):
    nd = len(shape)
    return pl.BlockSpec(shape, lambda *_: (0,) * nd, pipeline_mode=pl.Buffered(1))


def _rms(x, gain):
    return x * lax.rsqrt(jnp.mean(x * x, axis=-1, keepdims=True) + EPS) * gain


def _dot(a, b):
    return jnp.dot(a, b, preferred_element_type=F32)


def _sigmoid(x):
    return 1.0 / (1.0 + jnp.exp(-x))


def _gelu_tanh(x):
    c = math.sqrt(2.0 / math.pi)
    return x * (0.5 * (1.0 + jnp.tanh(c * (x + 0.044715 * (x * x * x)))))


def _swiglu_residual(x, gain, wg_ref, wu_ref, wd_ref):
    h = _rms(x, gain).astype(BF16)
    gate = _dot(h, wg_ref[...])
    up = _dot(h, wu_ref[...])
    a = (gate * _sigmoid(gate) * up).astype(BF16)
    return x + 0.5 * _dot(a, wd_ref[...])


WEIGHT_CHUNKS = 8
WEIGHT_DMA_DEPTH = 4


def _load_bf16_weights(layer, jobs, sem):
    depth = sem.shape[0]
    chunks = []
    for src, dst, stage in jobs:
        rows = stage.shape[1]
        for c in range(src.shape[1] // rows):
            chunks.append((src.at[layer, pl.ds(c * rows, rows), :], dst, c * rows, rows, stage))

    def copy(k):
        return pltpu.make_async_copy(chunks[k][0], chunks[k][4].at[k % depth], sem.at[k % depth])

    for k in range(min(depth - 1, len(chunks))):
        copy(k).start()
    for k, (_, dst, r0, rows, stage) in enumerate(chunks):
        if k + depth - 1 < len(chunks):
            copy(k + depth - 1).start()
        copy(k).wait()
        dst[r0:r0 + rows, :] = stage[k % depth].astype(BF16)


def _ffn_kernel(*refs, n_pre, final_norm, post_fn, n_post_in, n_post_out, layer):
    n_in = 5 + 2 * n_pre + int(final_norm) + n_post_in
    n_out = 1 + n_post_out
    x_ref = refs[0]
    pre = refs[1:1 + 2 * n_pre]
    g_ref, wg_hbm, wu_hbm, wd_hbm = refs[1 + 2 * n_pre:5 + 2 * n_pre]
    o_ref = refs[n_in]
    wg_sc, wu_sc, wd_sc, stage_wide, stage_narrow, sem = refs[n_in + n_out:]

    @pl.when(pl.program_id(0) == 0)
    def _():
        _load_bf16_weights(layer, [(wg_hbm, wg_sc, stage_wide), (wu_hbm, wu_sc, stage_wide),
                                   (wd_hbm, wd_sc, stage_narrow)], sem)

    n_sub = 2
    halves = []
    hm = x_ref.shape[0] // n_sub
    x_full = x_ref[...]
    for i in range(n_pre):
        x_full = x_full + _dot(pre[2 * i][...], pre[2 * i + 1][...])
    for t in range(n_sub):
        x = x_full[t * hm:(t + 1) * hm]
        yh = _swiglu_residual(x, g_ref[...], wg_sc, wu_sc, wd_sc)
        if final_norm:
            yh = _rms(yh, refs[5 + 2 * n_pre][...])
        halves.append(yh)
    y = jnp.concatenate(halves, axis=0)
    o_ref[...] = y
    if post_fn is not None:
        post_fn(y, *refs[n_in - n_post_in:n_in], *refs[n_in + 1:n_in + n_out])


def _ffn(x, pre, gain, wg, wu, wd, layer, final_gain=None, post=None, *, name="ffn"):
    S = x.shape[0]
    tm = min(S, FFN_TM)
    row = lambda w: pl.BlockSpec((tm, w), lambda i: (i, 0))
    hbm = pl.BlockSpec(memory_space=pl.ANY)
    args, specs = [x], [row(D_MODEL)]
    for a, w, blk in pre:
        args += [a, w]
        specs += [row(a.shape[1]),
                  pl.BlockSpec((a.shape[1], D_MODEL), lambda i, blk=blk: (blk, 0),
                               pipeline_mode=pl.Buffered(1))]
    args += [gain.reshape(1, D_MODEL), wg, wu, wd]
    specs += [_const_spec((1, D_MODEL)), hbm, hbm, hbm]
    if final_gain is not None:
        args.append(final_gain.reshape(1, D_MODEL))
        specs.append(_const_spec((1, D_MODEL)))
    post_fn, post_args, post_in, post_out, post_shapes = post or (None, [], [], [], [])
    out = pl.pallas_call(
        functools.partial(_ffn_kernel, n_pre=len(pre), final_norm=final_gain is not None,
                          post_fn=post_fn, n_post_in=len(post_args), n_post_out=len(post_out),
                          layer=layer),
        grid=(S // tm,),
        in_specs=specs + post_in,
        out_specs=[row(D_MODEL)] + post_out,
        out_shape=[jax.ShapeDtypeStruct((S, D_MODEL), F32)] + post_shapes,
        scratch_shapes=[
            pltpu.VMEM(wg.shape[1:], BF16), pltpu.VMEM(wu.shape[1:], BF16),
            pltpu.VMEM(wd.shape[1:], BF16),
            pltpu.VMEM((WEIGHT_DMA_DEPTH, wg.shape[1] // WEIGHT_CHUNKS, wg.shape[2]), F32),
            pltpu.VMEM((WEIGHT_DMA_DEPTH, wd.shape[1] // WEIGHT_CHUNKS, wd.shape[2]), F32),
            pltpu.SemaphoreType.DMA((WEIGHT_DMA_DEPTH,)),
        ],
        compiler_params=_params(1),
        name=name,
    )(*args, *post_args)
    return out if post else out[0]


def _ab_in_stage(x, g_ref, win_ref, cdft_ref, vn_ref, ws_ref, bs_ref, z_ref, gated_ref):
    h = _rms(x, g_ref[...]).astype(BF16)
    z_uv = _dot(h, win_ref[:, MIX_W:])
    z_f = _dot(h, win_ref[:, :MIX_W])
    for g in range(N_GROUPS):
        fg = z_f[:, g * LANES:(g + 1) * LANES].astype(BF16)
        ab = _dot(fg, cdft_ref[...])
        z_ref[0, :, g * LANES:(g + 1) * LANES] = ab[:, :LANES].astype(BF16)
        z_ref[1, :, g * LANES:(g + 1) * LANES] = ab[:, LANES:].astype(BF16)
    uv = _gelu_tanh(z_uv)
    for g in range(N_GROUPS):
        u = uv[:, g * LANES:(g + 1) * LANES]
        v = uv[:, MIX_W + g * LANES:MIX_W + (g + 1) * LANES]
        vn = _rms(v, vn_ref[:, g * LANES:(g + 1) * LANES]).astype(BF16)
        for c in range(x.shape[0] // LANES):
            rows = slice(c * LANES, (c + 1) * LANES)
            s = _dot(ws_ref[g], vn[rows]) + bs_ref[g]
            gated_ref[rows, g * LANES:(g + 1) * LANES] = (u[rows] * s).astype(BF16)


def _ab_in_post(S, gain, w_in, cdft, v_norm, w_s, b_full, *, tm=FFN_TM):
    args = [gain.reshape(1, D_MODEL), w_in, cdft, v_norm.reshape(1, MIX_W), w_s, b_full]
    in_specs = [_const_spec(a.shape) for a in args]
    out_specs = [pl.BlockSpec((2, tm, MIX_W), lambda i: (0, i, 0)),
                 pl.BlockSpec((tm, MIX_W), lambda i: (i, 0))]
    out_shapes = [jax.ShapeDtypeStruct((2, S, MIX_W), BF16),
                  jax.ShapeDtypeStruct((S, MIX_W), BF16)]
    return _ab_in_stage, args, in_specs, out_specs, out_shapes


def _dft1_kernel(w_ref, z_ref, t_ref):
    t_ref[...] = _dot(w_ref[...], z_ref[...]).astype(BF16)


def _dft1(w1, zv, *, bn=16384):
    rows, cols = zv.shape
    return pl.pallas_call(
        _dft1_kernel,
        grid=(cols // bn,),
        in_specs=[_const_spec(w1.shape), pl.BlockSpec((rows, bn), lambda i: (0, i))],
        out_specs=pl.BlockSpec((rows, bn), lambda i: (0, i)),
        out_shape=jax.ShapeDtypeStruct((rows, cols), BF16),
        compiler_params=_params(1),
        name="dft_stage1",
    )(w1, zv)


def _dft2_kernel(g_ref, t_ref, o_ref, *, kb):
    for i in range(kb):
        tcat = jnp.concatenate([t_ref[0, i], t_ref[1, i]], axis=0)
        o_ref[:, i * MIX_W:(i + 1) * MIX_W] = _dot(g_ref[i], tcat).astype(BF16)


def _dft2(g2, t4, *, kb=32):
    _, n1, n2, w = t4.shape
    kb = min(kb, n1)
    return pl.pallas_call(
        functools.partial(_dft2_kernel, kb=kb),
        grid=(n1 // kb,),
        in_specs=[
            pl.BlockSpec((kb, n2, 2 * n2), lambda i: (i, 0, 0)),
            pl.BlockSpec((2, kb, n2, w), lambda i: (0, i, 0, 0)),
        ],
        out_specs=pl.BlockSpec((n2, kb * w), lambda i: (0, i)),
        out_shape=jax.ShapeDtypeStruct((n2, n1 * w), BF16),
        compiler_params=_params(1),
        name="dft_stage2",
    )(g2, t4)


def _dft_tables(S):
    n1, n2 = S // LANES, LANES
    c = np.arange(LANES)
    ang = 2.0 * np.pi * np.outer(c, c) / LANES
    cdft = np.concatenate([np.cos(ang), np.sin(ang)], axis=1) / np.sqrt(LANES)
    a1 = 2.0 * np.pi * np.outer(np.arange(n1), np.arange(n1)) / n1
    c1, s1 = np.cos(a1), np.sin(a1)
    w1 = np.block([[c1, -s1], [-s1, -c1]]) / np.sqrt(n1)
    k = np.arange(n1)[:, None] + n1 * np.arange(n2)[None, :]
    a2 = 2.0 * np.pi * k[:, :, None] * np.arange(n2)[None, None, :] / S
    g2 = np.concatenate([np.cos(a2), np.sin(a2)], axis=-1) / np.sqrt(n2)
    to_bf16 = lambda a: jnp.asarray(a, dtype=F32).astype(BF16)
    return to_bf16(cdft), to_bf16(w1), to_bf16(g2)


def _qkv_kernel(x_ref, g_ref, w_ref, qn_ref, kn_ref, rowtab_ref, coltab_ref, q_ref, kt_ref, v_ref):
    tm = x_ref.shape[0]
    h = _rms(x_ref[...], g_ref[...]).astype(BF16)
    qkv = _dot(h, w_ref[...])
    rows_per_tile = tm // GRID_W
    row0 = pl.program_id(0) * rows_per_tile
    tab = jnp.concatenate(
        [jnp.broadcast_to(rowtab_ref[pl.ds(row0 + r, 1), :], coltab_ref.shape) + coltab_ref[...]
         for r in range(rows_per_tile)], axis=0)
    cosf = tab[:, :HEAD_DIM]
    sinf = tab[:, HEAD_DIM:]

    def head(col, gain):
        y = _rms(qkv[:, col:col + HEAD_DIM], gain)
        return y * cosf + pltpu.roll(y, HEAD_DIM // 2, 1) * sinf

    for n in range(N_HEADS):
        q_ref[:, n * HEAD_DIM:(n + 1) * HEAD_DIM] = (
            head(n * HEAD_DIM, qn_ref[...]) * Q_SCALE).astype(BF16)
    k0 = N_HEADS * HEAD_DIM
    v0 = k0 + N_KV_HEADS * HEAD_DIM
    for n in range(N_KV_HEADS):
        kt_ref[n * HEAD_DIM:(n + 1) * HEAD_DIM, :] = head(k0 + n * HEAD_DIM, kn_ref[...]).T.astype(BF16)
        vcol = 2 * n * HEAD_DIM
        v_ref[:, vcol:vcol + HEAD_DIM] = qkv[:, v0 + n * HEAD_DIM:v0 + (n + 1) * HEAD_DIM].astype(BF16)
        v_ref[:, vcol + HEAD_DIM:vcol + 2 * HEAD_DIM] = jnp.ones((tm, HEAD_DIM), BF16)


def _qkv(x, gain, w, qn, kn, rowtab, coltab, *, tm=256):
    S = x.shape[0]
    assert tm % GRID_W == 0
    row = lambda wd: pl.BlockSpec((tm, wd), lambda i: (i, 0))
    args = [gain.reshape(1, D_MODEL), w, qn.reshape(1, HEAD_DIM), kn.reshape(1, HEAD_DIM),
            rowtab, coltab]
    return pl.pallas_call(
        _qkv_kernel,
        grid=(S // tm,),
        in_specs=[row(D_MODEL)] + [_const_spec(a.shape) for a in args],
        out_specs=[row(N_HEADS * HEAD_DIM),
                   pl.BlockSpec((N_KV_HEADS * HEAD_DIM, tm), lambda i: (0, i)),
                   row(2 * N_KV_HEADS * HEAD_DIM)],
        out_shape=[jax.ShapeDtypeStruct((S, N_HEADS * HEAD_DIM), BF16),
                   jax.ShapeDtypeStruct((N_KV_HEADS * HEAD_DIM, S), BF16),
                   jax.ShapeDtypeStruct((S, 2 * N_KV_HEADS * HEAD_DIM), BF16)],
        compiler_params=_params(1),
        name="qkv_rope",
    )(x, *args)


def _flash_kernel(bounded_ref, q_ref, kt_ref, v_ref, o_ref, q_sc, s_sc, mx_sc, p_sc, al_sc, acc_sc,
                  *, tq, tk, nb):
    nk = kt_ref.shape[1] // tk
    n_tiles = q_ref.shape[0] // tq
    mq = KV_GROUP * tq
    n_chunks = n_tiles * nk
    assert nb == nk
    for t in range(n_tiles):
        for g in range(KV_GROUP):
            q_sc[t * mq + g * tq:t * mq + (g + 1) * tq, :] = (
                q_ref[t * tq:(t + 1) * tq, g * HEAD_DIM:(g + 1) * HEAD_DIM])

    def scores(jj, with_max=True):
        row = pl.multiple_of((jj // nk) * mq, mq)
        off = pl.multiple_of((jj % nk) * tk, tk)
        s = _dot(q_sc[pl.ds(row, mq), :], kt_ref[:, pl.ds(off, tk)])
        if not with_max:
            return s, None
        mx = s[:, :LANES]
        for t in range(1, tk // LANES):
            mx = jnp.maximum(mx, s[:, t * LANES:(t + 1) * LANES])
        return s, mx

    def probs(s, shift):
        return jnp.exp2((s - jnp.tile(shift, (1, tk // LANES))).astype(BF16))

    def pv(p, jj):
        off = pl.multiple_of((jj % nk) * tk, tk)
        return _dot(p, v_ref[pl.ds(off, tk), :])

    def write_out(acc, tile):
        o = acc[:, :HEAD_DIM] / acc[:, HEAD_DIM:]
        row = pl.multiple_of(tile * tq, tq)
        for g in range(KV_GROUP):
            o_ref[pl.ds(row, tq), g * HEAD_DIM:(g + 1) * HEAD_DIM] = (
                o[g * tq:(g + 1) * tq].astype(BF16))

    def sweep(bounded):
        s0, mx0 = scores(0)
        s_sc[...] = s0
        mx_sc[...] = mx0
        acc_sc[...] = jnp.concatenate(
            [jnp.zeros((mq, HEAD_DIM), F32), jnp.ones((mq, HEAD_DIM), F32)], axis=1)
        p_sc[...] = jnp.zeros(p_sc.shape, BF16)
        if not bounded:
            al_sc[...] = jnp.ones(al_sc.shape, F32)

        def body(i, carry):
            j0 = nb * i
            s, mx, p, acc = s_sc[...], mx_sc[...], p_sc[...], acc_sc[...]
            if bounded:
                shift = jnp.broadcast_to(jnp.max(mx, axis=-1, keepdims=True), (mq, LANES))
            else:
                m, alpha = jnp.full((mq, LANES), -1e30, F32), al_sc[...]
            for c in range(nb):
                jj = j0 + c
                last = c == nb - 1
                pvd = pv(p, jnp.maximum(jj - 1, 0) if c == 0 else jj - 1)
                if bounded:
                    acc = pvd if c == 1 else acc + pvd
                else:
                    acc = acc * jnp.tile(alpha, (1, 2)) + pvd
                if c == 0:
                    write_out(acc, jnp.maximum(jj - 1, 0) // nk)
                s_next, mx_next = scores(jnp.minimum(jj + 1, n_chunks - 1) if last else jj + 1,
                                         with_max=last or not bounded)
                if bounded:
                    p = probs(s, shift)
                else:
                    m_new = jnp.maximum(m, jnp.max(mx, axis=-1, keepdims=True))
                    alpha = jnp.exp2(m - m_new)
                    p = probs(s, m_new)
                    m = m_new
                s = s_next
                if mx_next is not None:
                    mx = mx_next
            s_sc[...], mx_sc[...], p_sc[...], acc_sc[...] = s, mx, p, acc
            if not bounded:
                al_sc[...] = alpha
            return carry

        lax.fori_loop(0, n_chunks // nb, body, 0)
        pvd = pv(p_sc[...], n_chunks - 1)
        acc = acc_sc[...] + pvd if bounded else acc_sc[...] * jnp.tile(al_sc[...], (1, 2)) + pvd
        write_out(acc, n_tiles - 1)

    @pl.when(bounded_ref[0] != 0)
    def _():
        sweep(True)

    @pl.when(bounded_ref[0] == 0)
    def _():
        sweep(False)


def _flash(q, k, v, bounded, *, tq=FLASH_TQ, tk=FLASH_TK, n_tiles=FLASH_TILES):
    S = q.shape[0]
    gw = KV_GROUP * HEAD_DIM
    mq = KV_GROUP * tq
    n_tiles = min(n_tiles, S // tq)
    assert S % tk == 0 and S % (n_tiles * tq) == 0
    return pl.pallas_call(
        functools.partial(_flash_kernel, tq=tq, tk=tk, nb=S // tk),
        grid=(N_KV_HEADS, S // (n_tiles * tq)),
        in_specs=[
            pl.BlockSpec(memory_space=pltpu.SMEM),
            pl.BlockSpec((n_tiles * tq, gw), lambda h, i: (i, h)),
            pl.BlockSpec((HEAD_DIM, S), lambda h, i: (h, 0)),
            pl.BlockSpec((S, 2 * HEAD_DIM), lambda h, i: (0, h)),
        ],
        out_specs=pl.BlockSpec((n_tiles * tq, gw), lambda h, i: (i, h)),
        out_shape=jax.ShapeDtypeStruct((S, N_HEADS * HEAD_DIM), BF16),
        scratch_shapes=[
            pltpu.VMEM((n_tiles * mq, HEAD_DIM), BF16),
            pltpu.VMEM((mq, tk), F32),
            pltpu.VMEM((mq, LANES), F32),
            pltpu.VMEM((mq, tk), BF16),
            pltpu.VMEM((mq, LANES), F32),
            pltpu.VMEM((mq, 2 * HEAD_DIM), F32),
        ],
        compiler_params=_params(2),
        name="gqa_flash",
    )(bounded, q, k, v)


def _rope_tables(S):
    inv_freq = ROPE_THETA ** (-jnp.arange(ROPE_FREQS, dtype=F32) / ROPE_FREQS)

    def axis_table(n, row_axis):
        ang = jnp.arange(n).astype(F32)[:, None] * inv_freq[None, :]
        cos, sin, zero = jnp.cos(ang), jnp.sin(ang), jnp.zeros_like(ang)
        pair = (lambda a: [a, zero]) if row_axis else (lambda a: [zero, a])
        return jnp.concatenate(pair(cos) + pair(cos) + pair(-sin) + pair(sin), axis=-1)

    return axis_table(S // GRID_W, True), axis_table(GRID_W, False)


def kernel(x, ffn1_norm, ffn1_w_gate, ffn1_w_up, ffn1_w_down, mix_norm, ab_w_in, ab_v_norm, ab_w_s, ab_b_s, ab_w_out, attn_w_qkv, attn_q_norm, attn_k_norm, attn_w_o, ffn2_norm, ffn2_w_gate, ffn2_w_up, ffn2_w_down, final_norm):
    B, S, _ = x.shape
    assert B == 1 and S % 2048 == 0
    bf = lambda w: w.astype(BF16)
    xs = x.reshape(S, D_MODEL)
    ffn1_w = (ffn1_w_gate, ffn1_w_up, ffn1_w_down)
    ffn2_w = (ffn2_w_gate, ffn2_w_up, ffn2_w_down)

    cdft, w1, g2 = _dft_tables(S)
    n1 = S // LANES
    b_full = jnp.broadcast_to(ab_b_s[0][:, :, None], (N_GROUPS, LANES, LANES))
    ab_in = _ab_in_post(S, mix_norm[0], bf(ab_w_in[0]), cdft, ab_v_norm[0], bf(ab_w_s[0]), b_full)
    xs, z, gated = _ffn(xs, [], ffn1_norm[0], *ffn1_w, 0, post=ab_in, name="ffn1_abin_l0")
    t = _dft1(w1, z.reshape(2 * n1, LANES * MIX_W))
    f = _dft2(g2, t.reshape(2, n1, LANES, MIX_W)).reshape(S, MIX_W)
    w_out = bf(ab_w_out[0])
    xs = _ffn(xs, [(f, w_out, 0), (gated, w_out, 1)], ffn2_norm[0], *ffn2_w, 0,
              name="mixout_ffn2_l0")

    half = np.concatenate([np.arange(0, HEAD_DIM, 2), np.arange(1, HEAD_DIM, 2)])
    n_rot = N_HEADS + N_KV_HEADS
    cols = np.concatenate([(h * HEAD_DIM + half) for h in range(n_rot)]
                          + [np.arange(n_rot * HEAD_DIM, (n_rot + N_KV_HEADS) * HEAD_DIM)])
    w_qkv = bf(attn_w_qkv[0][:, cols])
    xs = _ffn(xs, [], ffn1_norm[1], *ffn1_w, 1, name="ffn1_l1")
    q, k, v = _qkv(xs, mix_norm[1], w_qkv, attn_q_norm[0][half], attn_k_norm[0][half],
                   *_rope_tables(S))
    score_bound = (HEAD_DIM * Q_SCALE * 1.02 * jnp.max(jnp.abs(attn_q_norm[0]))
                   * jnp.max(jnp.abs(attn_k_norm[0])))
    bounded = (score_bound <= SCORE_BOUND).astype(jnp.int32).reshape(1)
    o = _flash(q, k, v, bounded)
    xs = _ffn(xs, [(o, bf(attn_w_o[0]), 0)], ffn2_norm[1], *ffn2_w, 1, final_gain=final_norm,
              name="attnout_ffn2_final")
    return xs.reshape(B, S, D_MODEL)
```
